```python
import jax, jax.numpy as jnp
from jax import lax
import numpy as np

D_MODEL = 2048
BATCH = 4
SEQ = 8192
DEPTH = 2
DEC_BATCH = 8
DEC_SEQ = 64
PAST_LEN = 1024

CHUNK = 64
N_EVEN = (DEPTH + 1) // 2
N_ODD = DEPTH // 2
D_FF = 5632
A_WIDTH = D_MODEL // 2
A_GROUPS = 4
A_GROUP_DIM = A_WIDTH // A_GROUPS
A_CHUNK = 128
B_WIDTH = D_MODEL // 2
B_CONV_WIDTH = 31
B_HIST = B_CONV_WIDTH - 1
C_WINDOWS = (2, 4, 8, 16)
C_GROUPS = 4
C_GROUP_DIM = D_MODEL // C_GROUPS
C_HIST = max(C_WINDOWS) - 1
MIX_IN = 2 * A_WIDTH + 2 * B_WIDTH
MIX_OUT = A_WIDTH + B_WIDTH
EPS = 1e-6

kernel_name = 'chunk_causal_gmlp_conformer_pool_stream_step'


def rms_norm(x, g):
    xf = x.astype(jnp.float32)
    y = xf * lax.rsqrt(jnp.mean(xf * xf, axis=-1, keepdims=True) + EPS)
    return (y * g.astype(jnp.float32)).astype(x.dtype)


def layer_norm(x, g, b):
    xf = x.astype(jnp.float32)
    mu = jnp.mean(xf, axis=-1, keepdims=True)
    d = xf - mu
    var = jnp.mean(d * d, axis=-1, keepdims=True)
    y = d * lax.rsqrt(var + EPS) * g.astype(jnp.float32) + b.astype(jnp.float32)
    return y.astype(x.dtype)


def modulate(xn, shift, scale):
    return xn * (1 + scale[:, None, :]) + shift[:, None, :]


def swiglu(x, wg, wu, wd):
    return (jax.nn.silu(x @ wg) * (x @ wu)) @ wd


def chunk_mask():
    i = jnp.arange(A_CHUNK)
    return (i[None, :] // CHUNK) <= (i[:, None] // CHUNK)


def spatial_gating(vn, ws, bs):
    w = jnp.where(chunk_mask()[None], ws, 0)
    bsz, T = vn.shape[0], vn.shape[1]
    if T >= A_CHUNK:
        vc = vn.reshape(bsz, T // A_CHUNK, A_CHUNK, A_GROUPS, A_GROUP_DIM)
        out = jnp.einsum('gij,bnjgc->bnigc', w, vc) + bs.T[None, None, :, :, None]
        return out.reshape(bsz, T, A_GROUPS, A_GROUP_DIM)
    return jnp.einsum('gij,bjgc->bigc', w[:, :T, :T], vn) + bs[:, :T].T[None, :, :, None]


def mixer_ab(hm, conv_hist, w_in, v_g, v_b, ws, bs, dw, ln_g, ln_b, w_out):
    bsz, T, _ = hm.shape
    proj = hm @ w_in
    a_u, a_v = jnp.split(jax.nn.gelu(proj[..., :2 * A_WIDTH]), 2, axis=-1)
    b_a, b_g = jnp.split(proj[..., 2 * A_WIDTH:], 2, axis=-1)
    vn = layer_norm(a_v, v_g, v_b)
    gate = spatial_gating(vn.reshape(bsz, T, A_GROUPS, A_GROUP_DIM), ws, bs)
    a_out = a_u * gate.reshape(bsz, T, A_WIDTH)
    glu = b_a * jax.nn.sigmoid(b_g)
    xpad = jnp.concatenate([conv_hist.astype(glu.dtype), glu], axis=1)
    conv = lax.conv_general_dilated(
        xpad, dw[:, None, :].astype(xpad.dtype), window_strides=(1,), padding='VALID',
        dimension_numbers=('NWC', 'WIO', 'NWC'), feature_group_count=B_WIDTH)
    b_out = jax.nn.silu(layer_norm(conv, ln_g, ln_b))
    y = jnp.concatenate([a_out, b_out], axis=-1) @ w_out
    return y, vn, xpad[:, -B_HIST:]


def mixer_c(hm, pool_hist, start_pos, w_grp, b_grp, scale):
    bsz, T, _ = hm.shape
    xpad = jnp.concatenate([pool_hist.astype(hm.dtype), hm], axis=1)
    csum = jnp.concatenate([jnp.zeros((bsz, 1, D_MODEL), jnp.float32),
                            jnp.cumsum(xpad.astype(jnp.float32), axis=1)], axis=1)
    end = csum[:, C_HIST + 1:]
    pos = start_pos + jnp.arange(T)
    pooled = []
    for g, w in enumerate(C_WINDOWS):
        sl = slice(g * C_GROUP_DIM, (g + 1) * C_GROUP_DIM)
        s = end[..., sl] - csum[:, C_HIST + 1 - w:C_HIST + 1 - w + T, sl]
        cnt = jnp.minimum(w, pos + 1).astype(jnp.float32)
        pooled.append(s / cnt[None, :, None])
    pooled = (jnp.concatenate(pooled, axis=-1) - hm.astype(jnp.float32)).astype(hm.dtype)
    pg = pooled.reshape(bsz, T, C_GROUPS, C_GROUP_DIM)
    y = jnp.einsum('btgc,gcd->btgd', pg, w_grp) + b_grp
    return y.reshape(bsz, T, D_MODEL) * scale, xpad[:, -C_HIST:]


def run_trunk(x, c, start_pos, conv_hist, pool_hist, p):
    h = x
    new_conv, new_pool, new_av = [], [], []
    for l in range(DEPTH):
        mod = (jax.nn.silu(c) @ p['ada_w'][l] + p['ada_b'][l]).reshape(c.shape[0], 3, 3, D_MODEL)
        hn = modulate(rms_norm(h, p['norm_g'][l, 0]), mod[:, 0, 0], mod[:, 0, 1])
        h = h + 0.5 * mod[:, 0, 2][:, None] * swiglu(
            hn, p['ffn_w_gate'][l, 0], p['ffn_w_up'][l, 0], p['ffn_w_down'][l, 0])
        hn = modulate(rms_norm(h, p['norm_g'][l, 1]), mod[:, 1, 0], mod[:, 1, 1])
        if l % 2 == 0:
            e = l // 2
            y, vn, ch = mixer_ab(hn, conv_hist[e], p['ab_w_in'][e], p['a_v_norm_g'][e],
                                 p['a_v_norm_b'][e], p['a_ws'][e], p['a_bs'][e], p['b_dw'][e],
                                 p['b_ln_g'][e], p['b_ln_b'][e], p['ab_w_out'][e])
            new_conv.append(ch)
            new_av.append(vn)
        else:
            o = l // 2
            y, ph = mixer_c(hn, pool_hist[o], start_pos, p['c_w_grp'][o], p['c_b_grp'][o],
                            p['c_scale'][o])
            new_pool.append(ph)
        h = h + mod[:, 1, 2][:, None] * y
        hn = modulate(rms_norm(h, p['norm_g'][l, 2]), mod[:, 2, 0], mod[:, 2, 1])
        h = h + 0.5 * mod[:, 2, 2][:, None] * swiglu(
            hn, p['ffn_w_gate'][l, 1], p['ffn_w_up'][l, 1], p['ffn_w_down'][l, 1])
    return rms_norm(h, p['final_norm_g']), jnp.stack(new_conv), jnp.stack(new_pool), jnp.stack(new_av)


def setup_inputs(seed: int = 0) -> dict:
    key = jax.random.key(seed)
    ks = jax.random.split(key, 25)
    f32 = jnp.float32

    def nrm(k, shape, s):
        return jax.random.normal(k, shape, f32) * s

    ada_b = nrm(ks[7], (DEPTH, 3, 3, D_MODEL), 0.02).at[:, :, 2].add(1.0).reshape(DEPTH, 9 * D_MODEL)
    return {
        'x_prompt': nrm(ks[0], (BATCH, SEQ, D_MODEL), 1.0),
        'x_sample': nrm(ks[1], (DEC_BATCH, DEC_SEQ, D_MODEL), 1.0),
        'c_prompt': nrm(ks[2], (BATCH, D_MODEL), 1.0),
        'c_sample': nrm(ks[3], (DEC_BATCH, D_MODEL), 1.0),
        'state_conv_b': nrm(ks[4], (N_EVEN, DEC_BATCH, B_HIST, B_WIDTH), 0.5),
        'state_pool_c': nrm(ks[5], (N_ODD, DEC_BATCH, C_HIST, D_MODEL), 1.0),
        'ada_w': nrm(ks[6], (DEPTH, D_MODEL, 9 * D_MODEL), 0.25 * D_MODEL ** -0.5),
        'ada_b': ada_b,
        'norm_g': 1.0 + nrm(ks[8], (DEPTH, 3, D_MODEL), 0.02),
        'final_norm_g': 1.0 + nrm(ks[9], (D_MODEL,), 0.02),
        'ffn_w_gate': nrm(ks[10], (DEPTH, 2, D_MODEL, D_FF), D_MODEL ** -0.5),
        'ffn_w_up': nrm(ks[11], (DEPTH, 2, D_MODEL, D_FF), D_MODEL ** -0.5),
        'ffn_w_down': nrm(ks[12], (DEPTH, 2, D_FF, D_MODEL), D_FF ** -0.5),
        'ab_w_in': nrm(ks[13], (N_EVEN, D_MODEL, MIX_IN), D_MODEL ** -0.5),
        'a_v_norm_g': 1.0 + nrm(ks[14], (N_EVEN, A_WIDTH), 0.02),
        'a_v_norm_b': nrm(ks[15], (N_EVEN, A_WIDTH), 0.02),
        'a_ws': nrm(ks[16], (N_EVEN, A_GROUPS, A_CHUNK, A_CHUNK), A_CHUNK ** -0.5),
        'a_bs': 1.0 + nrm(ks[17], (N_EVEN, A_GROUPS, A_CHUNK), 0.02),
        'b_dw': nrm(ks[18], (N_EVEN, B_CONV_WIDTH, B_WIDTH), B_CONV_WIDTH ** -0.5),
        'b_ln_g': 1.0 + nrm(ks[19], (N_EVEN, B_WIDTH), 0.02),
        'b_ln_b': nrm(ks[20], (N_EVEN, B_WIDTH), 0.02),
        'ab_w_out': nrm(ks[21], (N_EVEN, MIX_OUT, D_MODEL), MIX_OUT ** -0.5),
        'c_w_grp': nrm(ks[22], (N_ODD, C_GROUPS, C_GROUP_DIM, C_GROUP_DIM), C_GROUP_DIM ** -0.5),
        'c_b_grp': nrm(ks[23], (N_ODD, C_GROUPS, C_GROUP_DIM), 0.02),
        'c_scale': 1.0 + nrm(ks[24], (N_ODD, D_MODEL), 0.02),
    }


def reference(x_prompt, x_sample, c_prompt, c_sample, state_conv_b, state_pool_c,
              ada_w, ada_b, norm_g, final_norm_g, ffn_w_gate, ffn_w_up, ffn_w_down,
              ab_w_in, a_v_norm_g, a_v_norm_b, a_ws, a_bs, b_dw, b_ln_g, b_ln_b, ab_w_out,
              c_w_grp, c_b_grp, c_scale):
    p = {
        'ada_w': ada_w, 'ada_b': ada_b, 'norm_g': norm_g, 'final_norm_g': final_norm_g,
        'ffn_w_gate': ffn_w_gate, 'ffn_w_up': ffn_w_up, 'ffn_w_down': ffn_w_down,
        'ab_w_in': ab_w_in, 'a_v_norm_g': a_v_norm_g, 'a_v_norm_b': a_v_norm_b,
        'a_ws': a_ws, 'a_bs': a_bs, 'b_dw': b_dw, 'b_ln_g': b_ln_g, 'b_ln_b': b_ln_b,
        'ab_w_out': ab_w_out, 'c_w_grp': c_w_grp, 'c_b_grp': c_b_grp, 'c_scale': c_scale,
    }
    bp = x_prompt.shape[0]
    conv0 = jnp.zeros((N_EVEN, bp, B_HIST, B_WIDTH), x_prompt.dtype)
    pool0 = jnp.zeros((N_ODD, bp, C_HIST, D_MODEL), x_prompt.dtype)
    y_prompt, new_conv_prompt, new_pool_prompt, _ = run_trunk(
        x_prompt, c_prompt, 0, conv0, pool0, p)
    y_sample, new_conv_sample, new_pool_sample, new_av_sample = run_trunk(
        x_sample, c_sample, PAST_LEN, state_conv_b, state_pool_c, p)
    return (y_prompt, y_sample, new_conv_prompt, new_conv_sample,
            new_pool_prompt, new_pool_sample, new_av_sample)
```

```python
import functools

import jax
import jax.numpy as jnp
from jax import lax
from jax.experimental import pallas as pl
from jax.experimental.pallas import tpu as pltpu

EPS = 1e-6
STREAM_CHUNK = 64
POOL_WINDOWS = (2, 4, 8, 16)
PAST_LEN = 1024

V7X_SUBLANES = 8
V7X_VMEM_LIMIT_BYTES = 56 * 1024 * 1024

PROMPT_FFN_ROWS = 512
PROMPT_MIXER_ROWS = 256
FFN_HIDDEN_TILE = 512

F32 = jnp.float32
BF16 = jnp.bfloat16


def _round_up(n, m):
    return (n + m - 1) // m * m


def _rms_mod(h, g, shift, scale):
    y = h * lax.rsqrt(jnp.mean(h * h, axis=-1, keepdims=True) + EPS) * g
    return y * (1 + scale) + shift


def _layer_norm(x, g, b):
    mu = jnp.mean(x, axis=-1, keepdims=True)
    d = x - mu
    var = jnp.mean(d * d, axis=-1, keepdims=True)
    return d * lax.rsqrt(var + EPS) * g + b


def _dot(a, b):
    return jnp.dot(a, b, preferred_element_type=F32)


def _ada_kernel(c_ref, w_ref, b_ref, o_ref):
    sc = jax.nn.silu(c_ref[...]).astype(BF16)
    o_ref[...] = _dot(sc, w_ref[...].astype(BF16)) + b_ref[...]


def _ada_call(c_all, ada_w, ada_b):
    n_layers, d, nd = ada_w.shape
    n_comp = nd // d
    rows = c_all.shape[0]
    tn = d // 2
    per = d // tn
    return pl.pallas_call(
        _ada_kernel,
        grid=(n_layers, nd // tn),
        in_specs=[
            pl.BlockSpec((rows, d), lambda l, n: (0, 0)),
            pl.BlockSpec((None, d, tn), lambda l, n: (l, 0, n)),
            pl.BlockSpec((None, 1, tn), lambda l, n: (l, 0, n)),
        ],
        out_specs=pl.BlockSpec((None, None, rows, tn), lambda l, n: (l, n // per, 0, n % per)),
        out_shape=jax.ShapeDtypeStruct((n_layers, n_comp, rows, d), F32),
        compiler_params=pltpu.CompilerParams(
            dimension_semantics=("arbitrary", "arbitrary"),
            vmem_limit_bytes=V7X_VMEM_LIMIT_BYTES),
        name="ada_proj",
    )(c_all, ada_w, ada_b.reshape(n_layers, 1, nd))


def _ffn_kernel(h_ref, shift_ref, scale_ref, gate_ref, ng_ref, wg_ref, wu_ref, wd_ref, fg_ref,
                o_ref, hn_ref, *, final_norm):
    j = pl.program_id(2)
    bt, tt, d = h_ref.shape

    @pl.when(j == 0)
    def _():
        hn = _rms_mod(h_ref[...], ng_ref[...], shift_ref[...], scale_ref[...])
        hn_ref[...] = hn.astype(BF16).reshape(bt * tt, d)
        o_ref[...] = jnp.zeros_like(o_ref)

    hn = hn_ref[...]
    a = _dot(hn, wg_ref[...])
    b = _dot(hn, wu_ref[...])
    act = (jax.nn.silu(a) * b).astype(BF16)
    o_ref[...] += _dot(act, wd_ref[...]).reshape(bt, tt, d)

    @pl.when(j == pl.num_programs(2) - 1)
    def _():
        y = h_ref[...] + (0.5 * gate_ref[...]) * o_ref[...]
        if final_norm:
            y = y * lax.rsqrt(jnp.mean(y * y, axis=-1, keepdims=True) + EPS) * fg_ref[...]
        o_ref[...] = y


def _ffn_call(h, mod, mod_row, norm_g, wg, wu, wd, l, i, final_g, *, bt, tt, tf, final_norm):
    bsz, seq, d = h.shape
    ff = wg.shape[-1]
    grid = (bsz // bt, seq // tt, ff // tf)
    mspec = [pl.BlockSpec((bt, 1, d), functools.partial(_mod_index, row=mod_row(k), bt=bt))
             for k in range(3)]
    return pl.pallas_call(
        functools.partial(_ffn_kernel, final_norm=final_norm),
        grid=grid,
        in_specs=[
            pl.BlockSpec((bt, tt, d), lambda b, t, j: (b, t, 0)),
            *mspec,
            pl.BlockSpec((1, d), lambda b, t, j: (0, 0)),
            pl.BlockSpec((None, None, d, tf), lambda b, t, j: (l, i, 0, j)),
            pl.BlockSpec((None, None, d, tf), lambda b, t, j: (l, i, 0, j)),
            pl.BlockSpec((None, None, tf, d), lambda b, t, j: (l, i, j, 0)),
            pl.BlockSpec((1, d), lambda b, t, j: (0, 0)),
        ],
        out_specs=pl.BlockSpec((bt, tt, d), lambda b, t, j: (b, t, 0)),
        out_shape=jax.ShapeDtypeStruct(h.shape, F32),
        scratch_shapes=[pltpu.VMEM((bt * tt, d), BF16)],
        compiler_params=pltpu.CompilerParams(
            dimension_semantics=("arbitrary", "arbitrary", "arbitrary"),
            vmem_limit_bytes=V7X_VMEM_LIMIT_BYTES),
        name="ffn_half_step",
    )(h, mod, mod, mod, norm_g, wg, wu, wd, final_g)


def _mod_index(b, *_, row, bt):
    return (row // bt + b, 0, 0)


def _mixer_ab_kernel(h_ref, shift_ref, scale_ref, gate_ref, ng_ref, hist_ref, win_ref, vg_ref,
                     vb_ref, ws_ref, bst_ref, dw_ref, lng_ref, lnb_ref, wout_ref,
                     o_ref, convnew_ref, *rest, emit_vn, conv_rows, conv_cols):
    if emit_vn:
        vn_ref, xpad_ref, aout_ref, conv_ref, xsh_ref = rest
    else:
        xpad_ref, aout_ref, conv_ref, xsh_ref = rest
    t = pl.program_id(1)
    bt, tt, d = h_ref.shape
    aw = vg_ref.shape[-1]
    bw = lng_ref.shape[-1]
    n_grp, a_chunk, _ = ws_ref.shape
    gd = aw // n_grp
    n_taps = dw_ref.shape[0]
    hpad = xpad_ref.shape[1] - tt
    off = hpad - (n_taps - 1)
    cr = min(a_chunk, tt)

    @pl.when(t == 0)
    def _():
        xpad_ref[:, 0:hpad, :] = hist_ref[...]

    h = h_ref[...]
    hn = _rms_mod(h, ng_ref[...], shift_ref[...], scale_ref[...]).astype(BF16).reshape(bt * tt, d)

    u = jax.nn.gelu(_dot(hn, win_ref[:, 0:aw]))
    v = jax.nn.gelu(_dot(hn, win_ref[:, aw:2 * aw]))
    vn = _layer_norm(v, vg_ref[...], vb_ref[...])
    if emit_vn:
        vn_ref[...] = vn.reshape(bt, tt, aw)
    vnb = vn.astype(BF16)
    row = lax.broadcasted_iota(jnp.int32, (cr, cr), 0) // STREAM_CHUNK
    col = lax.broadcasted_iota(jnp.int32, (cr, cr), 1) // STREAM_CHUNK
    for g in range(n_grp):
        wm = jnp.where(col <= row, ws_ref[g, 0:cr, 0:cr], 0.0).astype(BF16)
        bias = bst_ref[0:cr, g:g + 1]
        for r0 in range(0, bt * tt, cr):
            gate = _dot(wm, vnb[r0:r0 + cr, g * gd:(g + 1) * gd]) + bias
            aout_ref[r0:r0 + cr, g * gd:(g + 1) * gd] = (
                u[r0:r0 + cr, g * gd:(g + 1) * gd] * gate).astype(BF16)

    ba = _dot(hn, win_ref[:, 2 * aw:2 * aw + bw])
    bg = _dot(hn, win_ref[:, 2 * aw + bw:2 * aw + 2 * bw])
    xpad_ref[:, hpad:hpad + tt, :] = (ba * jax.nn.sigmoid(bg)).reshape(bt, tt, bw)

    n_sh = xsh_ref.shape[1]
    for s in range(bt):
        for c0 in range(0, bw, conv_cols):
            cols = slice(c0, c0 + conv_cols)
            for r in range(1, V7X_SUBLANES):
                xsh_ref[r - 1] = xpad_ref[s, r:r + n_sh, cols]

            def conv_body(rb, carry, s=s, cols=cols):
                r0 = pl.multiple_of(rb * conv_rows, conv_rows)
                acc = jnp.zeros((conv_rows, conv_cols), F32)
                for k in range(n_taps):
                    r = (off + k) % V7X_SUBLANES
                    a = (off + k) - r
                    if r == 0:
                        x = xpad_ref[s, pl.ds(r0 + a, conv_rows), cols]
                    else:
                        x = xsh_ref[r - 1, pl.ds(r0 + a, conv_rows), :]
                    acc = acc + x * dw_ref[k:k + 1, cols]
                conv_ref[pl.ds(s * tt + r0, conv_rows), cols] = acc
                return carry

            lax.fori_loop(0, tt // conv_rows, conv_body, 0)
    conv = conv_ref[...]
    bout = jax.nn.silu(_layer_norm(conv, lng_ref[...], lnb_ref[...])).astype(BF16)

    y = _dot(aout_ref[...], wout_ref[0:aw, :]) + _dot(bout, wout_ref[aw:aw + bw, :])
    o_ref[...] = h + gate_ref[...] * y.reshape(bt, tt, d)

    tail = xpad_ref[:, tt:tt + hpad, :]
    xpad_ref[:, 0:hpad, :] = tail

    @pl.when(t == pl.num_programs(1) - 1)
    def _():
        convnew_ref[...] = tail


def _mixer_ab_call(h, mod, mod_row, norm_g, hist, w_in, v_g, v_b, ws, bs_t, dw, ln_g, ln_b, w_out,
                   *, bt, tt, emit_vn):
    bsz, seq, d = h.shape
    aw, bw = v_g.shape[-1], ln_g.shape[-1]
    hpad = hist.shape[1]
    conv_rows = min(tt, 32)
    conv_cols = 256
    once = dict(pipeline_mode=pl.Buffered(1))
    full2 = lambda a: pl.BlockSpec(a.shape, lambda b, t: (0, 0))
    mspec = [pl.BlockSpec((bt, 1, d), functools.partial(_mod_index, row=mod_row(k), bt=bt))
             for k in range(3)]
    out_specs = [pl.BlockSpec((bt, tt, d), lambda b, t: (b, t, 0)),
                 pl.BlockSpec((bt, hpad, bw), lambda b, t: (b, 0, 0))]
    out_shape = [jax.ShapeDtypeStruct(h.shape, F32),
                 jax.ShapeDtypeStruct((bsz, hpad, bw), F32)]
    if emit_vn:
        out_specs.append(pl.BlockSpec((bt, tt, aw), lambda b, t: (b, t, 0)))
        out_shape.append(jax.ShapeDtypeStruct((bsz, seq, aw), F32))
    return pl.pallas_call(
        functools.partial(_mixer_ab_kernel, emit_vn=emit_vn, conv_rows=conv_rows,
                          conv_cols=conv_cols),
        grid=(bsz // bt, seq // tt),
        in_specs=[
            pl.BlockSpec((bt, tt, d), lambda b, t: (b, t, 0)),
            *mspec,
            full2(norm_g),
            pl.BlockSpec((bt, hpad, bw), lambda b, t: (b, 0, 0)),
            pl.BlockSpec(w_in.shape, lambda b, t: (0, 0), **once),
            full2(v_g), full2(v_b),
            pl.BlockSpec(ws.shape, lambda b, t: (0, 0, 0)),
            full2(bs_t), full2(dw), full2(ln_g), full2(ln_b),
            pl.BlockSpec(w_out.shape, lambda b, t: (0, 0), **once),
        ],
        out_specs=out_specs,
        out_shape=out_shape,
        scratch_shapes=[pltpu.VMEM((bt, hpad + tt, bw), F32),
                        pltpu.VMEM((bt * tt, aw), BF16),
                        pltpu.VMEM((bt * tt, bw), F32),
                        pltpu.VMEM((V7X_SUBLANES - 1, hpad + tt - V7X_SUBLANES, conv_cols), F32)],
        compiler_params=pltpu.CompilerParams(
            dimension_semantics=("arbitrary", "arbitrary"),
            vmem_limit_bytes=V7X_VMEM_LIMIT_BYTES),
        name="mixer_ab",
    )(h, mod, mod, mod, norm_g, hist, w_in, v_g, v_b, ws, bs_t, dw, ln_g, ln_b, w_out)


def _mixer_c_kernel(h_ref, shift_ref, scale_ref, gate_ref, ng_ref, hist_ref, wgrp_ref, bgrp_ref,
                    cs_ref, o_ref, poolnew_ref, xpad_ref, y_ref, *, start_pos):
    t = pl.program_id(1)
    bt, tt, d = h_ref.shape
    n_grp = wgrp_ref.shape[0]
    gd = d // n_grp
    hpad = xpad_ref.shape[1] - tt

    @pl.when(t == 0)
    def _():
        xpad_ref[:, 0:hpad, :] = hist_ref[...]

    h = h_ref[...]
    hn = _rms_mod(h, ng_ref[...], shift_ref[...], scale_ref[...])
    xpad_ref[:, hpad:hpad + tt, :] = hn

    pos = start_pos + t * tt + lax.broadcasted_iota(jnp.int32, (1, tt, 1), 1)
    for g, w in enumerate(POOL_WINDOWS):
        cols = slice(g * gd, (g + 1) * gd)
        s = xpad_ref[:, hpad:hpad + tt, cols]
        for j in range(1, w):
            s = s + xpad_ref[:, hpad - j:hpad - j + tt, cols]
        cnt = jnp.minimum(w, pos + 1).astype(F32)
        pooled = (s / cnt - hn[:, :, cols]).astype(BF16).reshape(bt * tt, gd)
        y_ref[:, cols] = _dot(pooled, wgrp_ref[g])
    y = (y_ref[...] + bgrp_ref[...]) * cs_ref[...]
    o_ref[...] = h + gate_ref[...] * y.reshape(bt, tt, d)

    tail = xpad_ref[:, tt:tt + hpad, :]
    xpad_ref[:, 0:hpad, :] = tail

    @pl.when(t == pl.num_programs(1) - 1)
    def _():
        poolnew_ref[...] = tail


def _mixer_c_call(h, mod, mod_row, norm_g, hist, w_grp, b_grp, c_scale, *, bt, tt, start_pos):
    bsz, seq, d = h.shape
    hpad = hist.shape[1]
    full2 = lambda a: pl.BlockSpec(a.shape, lambda b, t: (0, 0))
    mspec = [pl.BlockSpec((bt, 1, d), functools.partial(_mod_index, row=mod_row(k), bt=bt))
             for k in range(3)]
    return pl.pallas_call(
        functools.partial(_mixer_c_kernel, start_pos=start_pos),
        grid=(bsz // bt, seq // tt),
        in_specs=[
            pl.BlockSpec((bt, tt, d), lambda b, t: (b, t, 0)),
            *mspec,
            full2(norm_g),
            pl.BlockSpec((bt, hpad, d), lambda b, t: (b, 0, 0)),
            pl.BlockSpec(w_grp.shape, lambda b, t: (0, 0, 0)),
            full2(b_grp), full2(c_scale),
        ],
        out_specs=[pl.BlockSpec((bt, tt, d), lambda b, t: (b, t, 0)),
                   pl.BlockSpec((bt, hpad, d), lambda b, t: (b, 0, 0))],
        out_shape=[jax.ShapeDtypeStruct(h.shape, F32),
                   jax.ShapeDtypeStruct((bsz, hpad, d), F32)],
        scratch_shapes=[pltpu.VMEM((bt, hpad + tt, d), F32),
                        pltpu.VMEM((bt * tt, d), F32)],
        compiler_params=pltpu.CompilerParams(
            dimension_semantics=("arbitrary", "arbitrary"),
            vmem_limit_bytes=V7X_VMEM_LIMIT_BYTES),
        name="mixer_c",
    )(h, mod, mod, mod, norm_g, hist, w_grp, b_grp, c_scale)


def _front_pad(hist, rows):
    return jnp.pad(hist, ((0, 0), (0, 0), (rows - hist.shape[2], 0), (0, 0)))


def _run_trunk(x, mod, row0, start_pos, conv_hist, pool_hist, p, *, bt, tt_ffn, tt_mix, tf,
               emit_vn):
    n_layers = p["norm_g"].shape[0]
    n_rows = mod.shape[2]
    d = x.shape[-1]
    mod_tab = mod.reshape(n_layers * 9 * n_rows, 1, d)
    n_conv = conv_hist.shape[2]
    n_pool = pool_hist.shape[2]
    conv_hist = _front_pad(conv_hist, _round_up(n_conv, V7X_SUBLANES))
    pool_hist = _front_pad(pool_hist, _round_up(n_pool, V7X_SUBLANES))

    h = x
    new_conv, new_pool, new_av = [], [], []
    for l in range(n_layers):
        def mod_row(sub, l=l):
            return lambda k: ((l * 9 + sub * 3 + k) * n_rows + row0)

        ffn = functools.partial(_ffn_call, wg=p["wg"], wu=p["wu"], wd=p["wd"], l=l,
                                final_g=p["final_norm_g"], bt=bt, tt=tt_ffn, tf=tf)
        h = ffn(h, mod_tab, mod_row(0), p["norm_g"][l, 0:1], i=0, final_norm=False)
        if l % 2 == 0:
            e = l // 2
            out = _mixer_ab_call(
                h, mod_tab, mod_row(1), p["norm_g"][l, 1:2], conv_hist[e], p["ab_w_in"][e],
                p["a_v_norm_g"][e:e + 1], p["a_v_norm_b"][e:e + 1], p["a_ws"][e], p["a_bs"][e].T,
                p["b_dw"][e], p["b_ln_g"][e:e + 1], p["b_ln_b"][e:e + 1], p["ab_w_out"][e],
                bt=bt, tt=tt_mix, emit_vn=emit_vn)
            h = out[0]
            new_conv.append(out[1][:, -n_conv:])
            if emit_vn:
                new_av.append(out[2])
        else:
            o = l // 2
            h, ph = _mixer_c_call(
                h, mod_tab, mod_row(1), p["norm_g"][l, 1:2], pool_hist[o], p["c_w_grp"][o],
                p["c_b_grp"][o].reshape(1, d), p["c_scale"][o:o + 1],
                bt=bt, tt=tt_mix, start_pos=start_pos)
            new_pool.append(ph[:, -n_pool:])
        h = ffn(h, mod_tab, mod_row(2), p["norm_g"][l, 2:3], i=1,
                final_norm=(l == n_layers - 1))
    return h, jnp.stack(new_conv), jnp.stack(new_pool), (jnp.stack(new_av) if emit_vn else None)


def kernel(x_prompt, x_sample, c_prompt, c_sample, state_conv_b, state_pool_c, ada_w, ada_b, norm_g, final_norm_g, ffn_w_gate, ffn_w_up, ffn_w_down, ab_w_in, a_v_norm_g, a_v_norm_b, a_ws, a_bs, b_dw, b_ln_g, b_ln_b, ab_w_out, c_w_grp, c_b_grp, c_scale):
    bp, sp, d = x_prompt.shape
    bs, ss, _ = x_sample.shape
    p = {
        "norm_g": norm_g, "final_norm_g": final_norm_g.reshape(1, d),
        "wg": ffn_w_gate.astype(BF16), "wu": ffn_w_up.astype(BF16), "wd": ffn_w_down.astype(BF16),
        "ab_w_in": ab_w_in.astype(BF16), "ab_w_out": ab_w_out.astype(BF16),
        "a_v_norm_g": a_v_norm_g, "a_v_norm_b": a_v_norm_b, "a_ws": a_ws, "a_bs": a_bs,
        "b_dw": b_dw, "b_ln_g": b_ln_g, "b_ln_b": b_ln_b,
        "c_w_grp": c_w_grp.astype(BF16), "c_b_grp": c_b_grp, "c_scale": c_scale,
    }
    n_rows = _round_up(bs + bp, V7X_SUBLANES)
    c_all = jnp.concatenate([c_sample, c_prompt, jnp.zeros((n_rows - bs - bp, d), F32)], axis=0)
    mod = _ada_call(c_all, ada_w, ada_b)

    n_even, n_odd = state_conv_b.shape[0], state_pool_c.shape[0]
    conv0 = jnp.zeros((n_even, bp) + state_conv_b.shape[2:], F32)
    pool0 = jnp.zeros((n_odd, bp) + state_pool_c.shape[2:], F32)
    y_p, conv_p, pool_p, _ = _run_trunk(
        x_prompt, mod, bs, 0, conv0, pool0, p,
        bt=1, tt_ffn=min(sp, PROMPT_FFN_ROWS), tt_mix=min(sp, PROMPT_MIXER_ROWS),
        tf=FFN_HIDDEN_TILE, emit_vn=False)
    y_s, conv_s, pool_s, av_s = _run_trunk(
        x_sample, mod, 0, PAST_LEN, state_conv_b, state_pool_c, p,
        bt=bs, tt_ffn=ss, tt_mix=ss, tf=FFN_HIDDEN_TILE, emit_vn=True)
    return (y_p, y_s, conv_p, conv_s, pool_p, pool_s, av_s)
```

```python
import functools

import jax
import jax.numpy as jnp
from jax import lax
from jax.experimental import pallas as pl
from jax.experimental.pallas import tpu as pltpu

EPS = 1e-6
STREAM_CHUNK = 64
POOL_WINDOWS = (2, 4, 8, 16)
PAST_LEN = 1024

V7X_SUBLANES = 8
V7X_MXU_DIM = 256
V7X_VMEM_LIMIT_BYTES = 56 * 1024 * 1024

PROMPT_FFN_ROWS = 1024
PROMPT_MIXER_ROWS = 256
FFN_HIDDEN_TILE = 512
MAX_UNROLLED_CONV_BLOCKS = 32

F32 = jnp.float32
BF16 = jnp.bfloat16


def _round_up(n, m):
    return (n + m - 1) // m * m


def _rms_mod(h, g, shift, scale):
    y = h * lax.rsqrt(jnp.mean(h * h, axis=-1, keepdims=True) + EPS) * g
    return y * (1 + scale) + shift


def _layer_norm(x, g, b):
    mu = jnp.mean(x, axis=-1, keepdims=True)
    d = x - mu
    var = jnp.mean(d * d, axis=-1, keepdims=True)
    return d * lax.rsqrt(var + EPS) * g + b


def _dot(a, b):
    return jnp.dot(a, b, preferred_element_type=F32)


def _ada_kernel(c_ref, w_ref, b_ref, o_ref):
    sc = jax.nn.silu(c_ref[...]).astype(BF16)
    o_ref[...] = _dot(sc, w_ref[...].astype(BF16)) + b_ref[...]


def _ada_call(c_all, ada_w, ada_b):
    n_layers, d, nd = ada_w.shape
    n_comp = nd // d
    rows = c_all.shape[0]
    tn = d // 2
    per = d // tn
    return pl.pallas_call(
        _ada_kernel,
        grid=(n_layers, nd // tn),
        in_specs=[
            pl.BlockSpec((rows, d), lambda l, n: (0, 0)),
            pl.BlockSpec((None, d, tn), lambda l, n: (l, 0, n)),
            pl.BlockSpec((None, 1, tn), lambda l, n: (l, 0, n)),
        ],
        out_specs=pl.BlockSpec((None, None, rows, tn), lambda l, n: (l, n // per, 0, n % per)),
        out_shape=jax.ShapeDtypeStruct((n_layers, n_comp, rows, d), F32),
        compiler_params=pltpu.CompilerParams(
            dimension_semantics=("arbitrary", "arbitrary"),
            vmem_limit_bytes=V7X_VMEM_LIMIT_BYTES),
        name="ada_proj",
    )(c_all, ada_w, ada_b.reshape(n_layers, 1, nd))


def _ffn_kernel(h_hbm, shift_ref, scale_ref, gate_ref, ng_ref, wg_ref, wu_ref, wd_ref, fg_ref,
                o_ref, hbuf_ref, hn_ref, sem, *, final_norm, norm_rows):
    b, t, j = pl.program_id(0), pl.program_id(1), pl.program_id(2)
    nb, nt, nj = pl.num_programs(0), pl.num_programs(1), pl.num_programs(2)
    bt, tt, d = o_ref.shape

    def h_copy(bb, ti):
        return pltpu.make_async_copy(
            h_hbm.at[pl.ds(bb * bt, bt), pl.ds(ti * tt, tt), :], hbuf_ref, sem)

    @pl.when(j == 0)
    def _():
        @pl.when((b == 0) & (t == 0))
        def _():
            h_copy(b, t).start()

        h_copy(b, t).wait()
        for s in range(bt):
            gmul = ng_ref[...] * (1 + scale_ref[s])
            shift = shift_ref[s]

            def rows(i, carry, s=s, gmul=gmul, shift=shift):
                r = pl.multiple_of(i * norm_rows, norm_rows)
                x = hbuf_ref[s, pl.ds(r, norm_rows), :]
                inv = lax.rsqrt(jnp.mean(x * x, axis=-1, keepdims=True) + EPS)
                hn_ref[pl.ds(s * tt + r, norm_rows), :] = ((x * inv) * gmul + shift).astype(BF16)
                o_ref[s, pl.ds(r, norm_rows), :] = x
                return carry

            lax.fori_loop(0, tt // norm_rows, rows, 0, unroll=4)

    @pl.when(j == jnp.minimum(1, nj - 1))
    def _():
        last_t = t == nt - 1

        @pl.when(jnp.logical_not(last_t & (b == nb - 1)))
        def _():
            h_copy(jnp.where(last_t, b + 1, b), jnp.where(last_t, 0, t + 1)).start()

    hn = hn_ref[...]
    a = _dot(hn, wg_ref[...])
    u = _dot(hn, wu_ref[...])
    act = (jax.nn.silu(a) * u).astype(BF16)
    o_ref[...] += (0.5 * gate_ref[...]) * _dot(act, wd_ref[...]).reshape(bt, tt, d)

    if final_norm:
        @pl.when(j == nj - 1)
        def _():
            for s in range(bt):
                def rows(i, carry, s=s):
                    r = pl.multiple_of(i * norm_rows, norm_rows)
                    y = o_ref[s, pl.ds(r, norm_rows), :]
                    inv = lax.rsqrt(jnp.mean(y * y, axis=-1, keepdims=True) + EPS)
                    o_ref[s, pl.ds(r, norm_rows), :] = (y * inv) * fg_ref[...]
                    return carry

                lax.fori_loop(0, tt // norm_rows, rows, 0, unroll=4)


def _ffn_call(h, mod, mod_row, norm_g, wg, wu, wd, l, i, final_g, *, bt, tt, tf, final_norm):
    bsz, seq, d = h.shape
    ff = wg.shape[-1]
    grid = (bsz // bt, seq // tt, ff // tf)
    mspec = [pl.BlockSpec((bt, 1, d), functools.partial(_mod_index, row=mod_row(k), bt=bt))
             for k in range(3)]
    norm_rows = 16
    assert seq % tt == 0 and bsz % bt == 0 and ff % tf == 0 and tt % (4 * norm_rows) == 0
    return pl.pallas_call(
        functools.partial(_ffn_kernel, final_norm=final_norm, norm_rows=norm_rows),
        grid=grid,
        in_specs=[
            pl.BlockSpec(memory_space=pl.ANY),
            *mspec,
            pl.BlockSpec((1, d), lambda b, t, j: (0, 0)),
            pl.BlockSpec((None, None, d, tf), lambda b, t, j: (l, i, 0, j)),
            pl.BlockSpec((None, None, d, tf), lambda b, t, j: (l, i, 0, j)),
            pl.BlockSpec((None, None, tf, d), lambda b, t, j: (l, i, j, 0)),
            pl.BlockSpec((1, d), lambda b, t, j: (0, 0)),
        ],
        out_specs=pl.BlockSpec((bt, tt, d), lambda b, t, j: (b, t, 0)),
        out_shape=jax.ShapeDtypeStruct(h.shape, F32),
        scratch_shapes=[pltpu.VMEM((bt, tt, d), F32),
                        pltpu.VMEM((bt * tt, d), BF16),
                        pltpu.SemaphoreType.DMA(())],
        compiler_params=pltpu.CompilerParams(
            dimension_semantics=("arbitrary", "arbitrary", "arbitrary"),
            vmem_limit_bytes=V7X_VMEM_LIMIT_BYTES),
        name="ffn_half_step",
    )(h, mod, mod, mod, norm_g, wg, wu, wd, final_g)


def _mod_index(b, *_, row, bt):
    return (row // bt + b, 0, 0)


def _mixer_ab_kernel(h_ref, shift_ref, scale_ref, gate_ref, ng_ref, hist_ref, win_ref, vg_ref,
                     vb_ref, ws_ref, bst_ref, dw_ref, lng_ref, lnb_ref, wout_ref,
                     o_ref, convnew_ref, *rest, emit_vn, conv_rows, conv_cols, unroll_conv):
    if emit_vn:
        vn_ref, xpad_ref, aout_ref, conv_ref, xsh_ref = rest
    else:
        xpad_ref, aout_ref, conv_ref, xsh_ref = rest
    t = pl.program_id(1)
    bt, tt, d = h_ref.shape
    aw = vg_ref.shape[-1]
    bw = lng_ref.shape[-1]
    n_grp, a_chunk, _ = ws_ref.shape
    gd = aw // n_grp
    n_taps = dw_ref.shape[0]
    hpad = xpad_ref.shape[1] - tt
    off = hpad - (n_taps - 1)
    cr = min(a_chunk, tt)

    @pl.when(t == 0)
    def _():
        xpad_ref[:, 0:hpad, :] = hist_ref[...]

    h = h_ref[...]
    hn = _rms_mod(h, ng_ref[...], shift_ref[...], scale_ref[...]).astype(BF16).reshape(bt * tt, d)

    ba = _dot(hn, win_ref[:, 2 * aw:2 * aw + bw])
    bg = _dot(hn, win_ref[:, 2 * aw + bw:2 * aw + 2 * bw])
    xpad_ref[:, hpad:hpad + tt, :] = (ba * jax.nn.sigmoid(bg)).reshape(bt, tt, bw)

    n_sh = xsh_ref.shape[1]

    def conv_block(s, r0, cols):
        acc = jnp.zeros((conv_rows, conv_cols), F32)
        for k in range(n_taps):
            r = (off + k) % V7X_SUBLANES
            a = (off + k) - r
            if r == 0:
                x = xpad_ref[s, pl.ds(r0 + a, conv_rows), cols]
            else:
                x = xsh_ref[r - 1, pl.ds(r0 + a, conv_rows), cols]
            acc = acc + x * dw_ref[k:k + 1, cols]
        conv_ref[pl.ds(s * tt + r0, conv_rows), cols] = acc

    for s in range(bt):
        for r in range(1, V7X_SUBLANES):
            xsh_ref[r - 1] = xpad_ref[s, r:r + n_sh, :]
        for c0 in range(0, bw, conv_cols):
            cols = slice(c0, c0 + conv_cols)
            if unroll_conv:
                for r0 in range(0, tt, conv_rows):
                    conv_block(s, r0, cols)
            else:
                def conv_body(rb, carry, s=s, cols=cols):
                    conv_block(s, pl.multiple_of(rb * conv_rows, conv_rows), cols)
                    return carry

                lax.fori_loop(0, tt // conv_rows, conv_body, 0)

    u = jax.nn.gelu(_dot(hn, win_ref[:, 0:aw]))
    v = jax.nn.gelu(_dot(hn, win_ref[:, aw:2 * aw]))
    vn = _layer_norm(v, vg_ref[...], vb_ref[...])
    if emit_vn:
        vn_ref[...] = vn.reshape(bt, tt, aw)
    vnb = vn.astype(BF16)
    gate_rows = min(V7X_MXU_DIM, bt * tt)
    row = lax.broadcasted_iota(jnp.int32, (a_chunk, a_chunk), 0)
    col = lax.broadcasted_iota(jnp.int32, (a_chunk, a_chunk), 1)
    keep = (row < cr) & (col < cr) & (col // STREAM_CHUNK <= row // STREAM_CHUNK)
    for g in range(n_grp):
        corner = jnp.where(keep, ws_ref[g], 0.0)
        diag = corner
        for i in range(1, a_chunk // cr):
            diag = diag + pltpu.roll(pltpu.roll(corner, i * cr, 0), i * cr, 1)
        diag = diag.astype(BF16)
        nb = gate_rows // a_chunk
        zero = jnp.zeros_like(diag)
        wbd = diag if nb <= 1 else jnp.concatenate(
            [jnp.concatenate([diag if i == j else zero for j in range(nb)], axis=1)
             for i in range(nb)], axis=0)
        wbd = wbd[0:gate_rows, 0:gate_rows]
        bias = jnp.concatenate([bst_ref[0:cr, g:g + 1]] * (gate_rows // cr), axis=0)
        for r0 in range(0, bt * tt, gate_rows):
            gate = _dot(wbd, vnb[r0:r0 + gate_rows, g * gd:(g + 1) * gd]) + bias
            aout_ref[r0:r0 + gate_rows, g * gd:(g + 1) * gd] = (
                u[r0:r0 + gate_rows, g * gd:(g + 1) * gd] * gate).astype(BF16)

    conv = conv_ref[...]
    bout = jax.nn.silu(_layer_norm(conv, lng_ref[...], lnb_ref[...])).astype(BF16)

    y = _dot(aout_ref[...], wout_ref[0:aw, :]) + _dot(bout, wout_ref[aw:aw + bw, :])
    o_ref[...] = h + gate_ref[...] * y.reshape(bt, tt, d)

    tail = xpad_ref[:, tt:tt + hpad, :]
    xpad_ref[:, 0:hpad, :] = tail

    @pl.when(t == pl.num_programs(1) - 1)
    def _():
        convnew_ref[...] = tail


def _mixer_ab_call(h, mod, mod_row, norm_g, hist, w_in, v_g, v_b, ws, bs_t, dw, ln_g, ln_b, w_out,
                   *, bt, tt, emit_vn):
    bsz, seq, d = h.shape
    aw, bw = v_g.shape[-1], ln_g.shape[-1]
    hpad = hist.shape[1]
    conv_rows = min(tt, 32)
    conv_cols = 256
    unroll_conv = bt * (bw // conv_cols) * (tt // conv_rows) <= MAX_UNROLLED_CONV_BLOCKS
    once = dict(pipeline_mode=pl.Buffered(1))
    full2 = lambda a: pl.BlockSpec(a.shape, lambda b, t: (0, 0))
    mspec = [pl.BlockSpec((bt, 1, d), functools.partial(_mod_index, row=mod_row(k), bt=bt))
             for k in range(3)]
    out_specs = [pl.BlockSpec((bt, tt, d), lambda b, t: (b, t, 0)),
                 pl.BlockSpec((bt, hpad, bw), lambda b, t: (b, 0, 0))]
    out_shape = [jax.ShapeDtypeStruct(h.shape, F32),
                 jax.ShapeDtypeStruct((bsz, hpad, bw), F32)]
    if emit_vn:
        out_specs.append(pl.BlockSpec((bt, tt, aw), lambda b, t: (b, t, 0)))
        out_shape.append(jax.ShapeDtypeStruct((bsz, seq, aw), F32))
    return pl.pallas_call(
        functools.partial(_mixer_ab_kernel, emit_vn=emit_vn, conv_rows=conv_rows,
                          conv_cols=conv_cols, unroll_conv=unroll_conv),
        grid=(bsz // bt, seq // tt),
        in_specs=[
            pl.BlockSpec((bt, tt, d), lambda b, t: (b, t, 0)),
            *mspec,
            full2(norm_g),
            pl.BlockSpec((bt, hpad, bw), lambda b, t: (b, 0, 0)),
            pl.BlockSpec(w_in.shape, lambda b, t: (0, 0), **once),
            full2(v_g), full2(v_b),
            pl.BlockSpec(ws.shape, lambda b, t: (0, 0, 0)),
            full2(bs_t), full2(dw), full2(ln_g), full2(ln_b),
            pl.BlockSpec(w_out.shape, lambda b, t: (0, 0), **once),
        ],
        out_specs=out_specs,
        out_shape=out_shape,
        scratch_shapes=[pltpu.VMEM((bt, hpad + tt, bw), F32),
                        pltpu.VMEM((bt * tt, aw), BF16),
                        pltpu.VMEM((bt * tt, bw), F32),
                        pltpu.VMEM((V7X_SUBLANES - 1, hpad + tt - V7X_SUBLANES, bw), F32)],
        compiler_params=pltpu.CompilerParams(
            dimension_semantics=("arbitrary", "arbitrary"),
            vmem_limit_bytes=V7X_VMEM_LIMIT_BYTES),
        name="mixer_ab",
    )(h, mod, mod, mod, norm_g, hist, w_in, v_g, v_b, ws, bs_t, dw, ln_g, ln_b, w_out)


def _mixer_c_kernel(h_ref, shift_ref, scale_ref, gate_ref, ng_ref, hist_ref, wgrp_ref, bgrp_ref,
                    cs_ref, o_ref, poolnew_ref, xpad_ref, y_ref, *, start_pos):
    t = pl.program_id(1)
    bt, tt, d = h_ref.shape
    n_grp = wgrp_ref.shape[0]
    gd = d // n_grp
    hpad = xpad_ref.shape[1] - tt

    @pl.when(t == 0)
    def _():
        xpad_ref[:, 0:hpad, :] = hist_ref[...]

    h = h_ref[...]
    hn = _rms_mod(h, ng_ref[...], shift_ref[...], scale_ref[...])
    xpad_ref[:, hpad:hpad + tt, :] = hn

    pos = start_pos + t * tt + lax.broadcasted_iota(jnp.int32, (1, tt, 1), 1)
    for g, w in enumerate(POOL_WINDOWS):
        cols = slice(g * gd, (g + 1) * gd)
        s = xpad_ref[:, hpad:hpad + tt, cols]
        for j in range(1, w):
            s = s + xpad_ref[:, hpad - j:hpad - j + tt, cols]
        cnt = jnp.minimum(w, pos + 1).astype(F32)
        pooled = (s / cnt - hn[:, :, cols]).astype(BF16).reshape(bt * tt, gd)
        y_ref[:, cols] = _dot(pooled, wgrp_ref[g])
    y = (y_ref[...] + bgrp_ref[...]) * cs_ref[...]
    o_ref[...] = h + gate_ref[...] * y.reshape(bt, tt, d)

    tail = xpad_ref[:, tt:tt + hpad, :]
    xpad_ref[:, 0:hpad, :] = tail

    @pl.when(t == pl.num_programs(1) - 1)
    def _():
        poolnew_ref[...] = tail


def _mixer_c_call(h, mod, mod_row, norm_g, hist, w_grp, b_grp, c_scale, *, bt, tt, start_pos):
    bsz, seq, d = h.shape
    hpad = hist.shape[1]
    full2 = lambda a: pl.BlockSpec(a.shape, lambda b, t: (0, 0))
    mspec = [pl.BlockSpec((bt, 1, d), functools.partial(_mod_index, row=mod_row(k), bt=bt))
             for k in range(3)]
    return pl.pallas_call(
        functools.partial(_mixer_c_kernel, start_pos=start_pos),
        grid=(bsz // bt, seq // tt),
        in_specs=[
            pl.BlockSpec((bt, tt, d), lambda b, t: (b, t, 0)),
            *mspec,
            full2(norm_g),
            pl.BlockSpec((bt, hpad, d), lambda b, t: (b, 0, 0)),
            pl.BlockSpec(w_grp.shape, lambda b, t: (0, 0, 0)),
            full2(b_grp), full2(c_scale),
        ],
        out_specs=[pl.BlockSpec((bt, tt, d), lambda b, t: (b, t, 0)),
                   pl.BlockSpec((bt, hpad, d), lambda b, t: (b, 0, 0))],
        out_shape=[jax.ShapeDtypeStruct(h.shape, F32),
                   jax.ShapeDtypeStruct((bsz, hpad, d), F32)],
        scratch_shapes=[pltpu.VMEM((bt, hpad + tt, d), F32),
                        pltpu.VMEM((bt * tt, d), F32)],
        compiler_params=pltpu.CompilerParams(
            dimension_semantics=("arbitrary", "arbitrary"),
            vmem_limit_bytes=V7X_VMEM_LIMIT_BYTES),
        name="mixer_c",
    )(h, mod, mod, mod, norm_g, hist, w_grp, b_grp, c_scale)


def _front_pad(hist, rows):
    return jnp.pad(hist, ((0, 0), (0, 0), (rows - hist.shape[2], 0), (0, 0)))


def _run_trunk(x, mod, row0, start_pos, conv_hist, pool_hist, p, *, bt, tt_ffn, tt_mix, tf,
               emit_vn):
    n_layers = p["norm_g"].shape[0]
    n_rows = mod.shape[2]
    d = x.shape[-1]
    mod_tab = mod.reshape(n_layers * 9 * n_rows, 1, d)
    n_conv = conv_hist.shape[2]
    n_pool = pool_hist.shape[2]
    conv_hist = _front_pad(conv_hist, _round_up(n_conv, V7X_SUBLANES))
    pool_hist = _front_pad(pool_hist, _round_up(n_pool, V7X_SUBLANES))

    h = x
    new_conv, new_pool, new_av = [], [], []
    for l in range(n_layers):
        def mod_row(sub, l=l):
            return lambda k: ((l * 9 + sub * 3 + k) * n_rows + row0)

        ffn = functools.partial(_ffn_call, wg=p["wg"], wu=p["wu"], wd=p["wd"], l=l,
                                final_g=p["final_norm_g"], bt=bt, tt=tt_ffn, tf=tf)
        h = ffn(h, mod_tab, mod_row(0), p["norm_g"][l, 0:1], i=0, final_norm=False)
        if l % 2 == 0:
            e = l // 2
            out = _mixer_ab_call(
                h, mod_tab, mod_row(1), p["norm_g"][l, 1:2], conv_hist[e], p["ab_w_in"][e],
                p["a_v_norm_g"][e:e + 1], p["a_v_norm_b"][e:e + 1], p["a_ws"][e], p["a_bs"][e].T,
                p["b_dw"][e], p["b_ln_g"][e:e + 1], p["b_ln_b"][e:e + 1], p["ab_w_out"][e],
                bt=bt, tt=tt_mix, emit_vn=emit_vn)
            h = out[0]
            new_conv.append(out[1][:, -n_conv:])
            if emit_vn:
                new_av.append(out[2])
        else:
            o = l // 2
            h, ph = _mixer_c_call(
                h, mod_tab, mod_row(1), p["norm_g"][l, 1:2], pool_hist[o], p["c_w_grp"][o],
                p["c_b_grp"][o].reshape(1, d), p["c_scale"][o:o + 1],
                bt=bt, tt=tt_mix, start_pos=start_pos)
            new_pool.append(ph[:, -n_pool:])
        h = ffn(h, mod_tab, mod_row(2), p["norm_g"][l, 2:3], i=1,
                final_norm=(l == n_layers - 1))
    return h, jnp.stack(new_conv), jnp.stack(new_pool), (jnp.stack(new_av) if emit_vn else None)


def kernel(x_prompt, x_sample, c_prompt, c_sample, state_conv_b, state_pool_c, ada_w, ada_b, norm_g, final_norm_g, ffn_w_gate, ffn_w_up, ffn_w_down, ab_w_in, a_v_norm_g, a_v_norm_b, a_ws, a_bs, b_dw, b_ln_g, b_ln_b, ab_w_out, c_w_grp, c_b_grp, c_scale):
    bp, sp, d = x_prompt.shape
    bs, ss, _ = x_sample.shape
    p = {
        "norm_g": norm_g, "final_norm_g": final_norm_g.reshape(1, d),
        "wg": ffn_w_gate.astype(BF16), "wu": ffn_w_up.astype(BF16), "wd": ffn_w_down.astype(BF16),
        "ab_w_in": ab_w_in.astype(BF16), "ab_w_out": ab_w_out.astype(BF16),
        "a_v_norm_g": a_v_norm_g, "a_v_norm_b": a_v_norm_b, "a_ws": a_ws, "a_bs": a_bs,
        "b_dw": b_dw, "b_ln_g": b_ln_g, "b_ln_b": b_ln_b,
        "c_w_grp": c_w_grp.astype(BF16), "c_b_grp": c_b_grp, "c_scale": c_scale,
    }
    n_rows = _round_up(bs + bp, V7X_SUBLANES)
    c_all = jnp.concatenate([c_sample, c_prompt, jnp.zeros((n_rows - bs - bp, d), F32)], axis=0)
    mod = _ada_call(c_all, ada_w, ada_b)

    n_even, n_odd = state_conv_b.shape[0], state_pool_c.shape[0]
    conv0 = jnp.zeros((n_even, bp) + state_conv_b.shape[2:], F32)
    pool0 = jnp.zeros((n_odd, bp) + state_pool_c.shape[2:], F32)
    y_p, conv_p, pool_p, _ = _run_trunk(
        x_prompt, mod, bs, 0, conv0, pool0, p,
        bt=1, tt_ffn=min(sp, PROMPT_FFN_ROWS), tt_mix=min(sp, PROMPT_MIXER_ROWS),
        tf=FFN_HIDDEN_TILE, emit_vn=False)
    y_s, conv_s, pool_s, av_s = _run_trunk(
        x_sample, mod, 0, PAST_LEN, state_conv_b, state_pool_c, p,
        bt=bs, tt_ffn=ss, tt_mix=ss, tf=FFN_HIDDEN_TILE, emit_vn=True)
    return (y_p, y_s, conv_p, conv_s, pool_p, pool_s, av_s)
```

```python
import functools

import jax
import jax.numpy as jnp
from jax import lax
from jax.experimental import pallas as pl
from jax.experimental.pallas import tpu as pltpu

EPS = 1e-6
STREAM_CHUNK = 64
POOL_WINDOWS = (2, 4, 8, 16)
PAST_LEN = 1024

V7X_SUBLANES = 8
V7X_MXU_DIM = 256
V7X_VMEM_LIMIT_BYTES = 56 * 1024 * 1024

PROMPT_FFN_ROWS = 1024
PROMPT_MIXER_ROWS = 256
FFN_HIDDEN_TILE = 512
SAMPLE_FFN_HIDDEN_TILE = 256
MAX_UNROLLED_CONV_BLOCKS = 32

F32 = jnp.float32
BF16 = jnp.bfloat16


def _round_up(n, m):
    return (n + m - 1) // m * m


def _rms_mod(h, g, shift, scale):
    y = h * lax.rsqrt(jnp.mean(h * h, axis=-1, keepdims=True) + EPS) * g
    return y * (1 + scale) + shift


def _layer_norm(x, g, b):
    mu = jnp.mean(x, axis=-1, keepdims=True)
    d = x - mu
    var = jnp.mean(d * d, axis=-1, keepdims=True)
    return d * lax.rsqrt(var + EPS) * g + b


def _dot(a, b):
    return jnp.dot(a, b, preferred_element_type=F32)


def _sigmoid(x):
    return 0.5 * jnp.tanh(0.5 * x) + 0.5


def _silu(x):
    return x * _sigmoid(x)


def _ada_kernel(c_ref, w_ref, b_ref, o_ref):
    sc = _silu(c_ref[...]).astype(BF16)
    o_ref[...] = _dot(sc, w_ref[...].astype(BF16)) + b_ref[...]


def _ada_call(c_all, ada_w, ada_b):
    n_layers, d, nd = ada_w.shape
    n_comp = nd // d
    rows = c_all.shape[0]
    tn = d // 2
    per = d // tn
    return pl.pallas_call(
        _ada_kernel,
        grid=(n_layers, nd // tn),
        in_specs=[
            pl.BlockSpec((rows, d), lambda l, n: (0, 0)),
            pl.BlockSpec((None, d, tn), lambda l, n: (l, 0, n)),
            pl.BlockSpec((None, 1, tn), lambda l, n: (l, 0, n)),
        ],
        out_specs=pl.BlockSpec((None, None, rows, tn), lambda l, n: (l, n // per, 0, n % per)),
        out_shape=jax.ShapeDtypeStruct((n_layers, n_comp, rows, d), F32),
        compiler_params=pltpu.CompilerParams(
            dimension_semantics=("arbitrary", "arbitrary"),
            vmem_limit_bytes=V7X_VMEM_LIMIT_BYTES),
        name="ada_proj",
    )(c_all, ada_w, ada_b.reshape(n_layers, 1, nd))


def _ffn_kernel(h_hbm, shift_ref, scale_ref, gate_ref, ng_ref, wg_ref, wu_ref, wd_ref, fg_ref,
                o_ref, *rest, final_norm, norm_rows, emit_bf16_weights):
    if emit_bf16_weights:
        wg_out, wu_out, wd_out, hbuf_ref, hn_ref, sem = rest
    else:
        hbuf_ref, hn_ref, sem = rest
    b, t, j = pl.program_id(0), pl.program_id(1), pl.program_id(2)
    nb, nt, nj = pl.num_programs(0), pl.num_programs(1), pl.num_programs(2)
    bt, tt, d = o_ref.shape
    n_chunks = tt // norm_rows

    def h_copy(bb, ti):
        return pltpu.make_async_copy(
            h_hbm.at[pl.ds(bb * bt, bt), pl.ds(ti * tt, tt), :], hbuf_ref, sem)

    @pl.when(j == 0)
    def _():
        @pl.when((b == 0) & (t == 0))
        def _():
            h_copy(b, t).start()

        h_copy(b, t).wait()
        for s in range(bt):
            gmul = ng_ref[...] * (1 + scale_ref[s])
            shift = shift_ref[s]

            def rows(i, carry, s=s, gmul=gmul, shift=shift):
                r = pl.multiple_of(i * norm_rows, norm_rows)
                x = hbuf_ref[s, pl.ds(r, norm_rows), :]
                inv = lax.rsqrt(jnp.mean(x * x, axis=-1, keepdims=True) + EPS)
                hn_ref[pl.ds(s * tt + r, norm_rows), :] = ((x * inv) * gmul + shift).astype(BF16)
                o_ref[s, pl.ds(r, norm_rows), :] = x
                return carry

            lax.fori_loop(0, n_chunks, rows, 0, unroll=4)

    @pl.when(j == jnp.minimum(1, nj - 1))
    def _():
        last_t = t == nt - 1

        @pl.when(jnp.logical_not(last_t & (b == nb - 1)))
        def _():
            h_copy(jnp.where(last_t, b + 1, b), jnp.where(last_t, 0, t + 1)).start()

    if emit_bf16_weights:
        wg, wu, wd = (w[...].astype(BF16) for w in (wg_ref, wu_ref, wd_ref))
        wg_out[...], wu_out[...], wd_out[...] = wg, wu, wd
    else:
        wg, wu, wd = wg_ref[...], wu_ref[...], wd_ref[...]
    hn = hn_ref[...]
    act = (_silu(_dot(hn, wg)) * _dot(hn, wu)).astype(BF16)
    o_ref[...] += (0.5 * gate_ref[...]) * _dot(act, wd).reshape(bt, tt, d)

    if final_norm:
        @pl.when(j == nj - 1)
        def _():
            y = o_ref[...]
            o_ref[...] = (y * lax.rsqrt(jnp.mean(y * y, axis=-1, keepdims=True) + EPS)) * fg_ref[...]


def _ffn_call(h, mod, mod_row, norm_g, weights, final_g, *, bt, tt, tf, final_norm):
    bsz, seq, d = h.shape
    kind, (wg, wu, wd) = weights[0], weights[1]
    ff = wg.shape[-1]
    grid = (bsz // bt, seq // tt, ff // tf)
    mspec = [pl.BlockSpec((bt, 1, d), functools.partial(_mod_index, row=mod_row(k), bt=bt))
             for k in range(3)]
    norm_rows = 16
    assert seq % tt == 0 and bsz % bt == 0 and ff % tf == 0 and tt % (4 * norm_rows) == 0
    up_spec = pl.BlockSpec((d, tf), lambda b, t, j: (0, j))
    down_spec = pl.BlockSpec((tf, d), lambda b, t, j: (j, 0))
    out_specs = [pl.BlockSpec((bt, tt, d), lambda b, t, j: (b, t, 0))]
    out_shape = [jax.ShapeDtypeStruct(h.shape, F32)]
    if kind == "f32":
        l, i = weights[2], weights[3]
        assert grid[:2] == (1, 1)
        w_specs = [pl.BlockSpec((None, None, d, tf), lambda b, t, j: (l, i, 0, j)),
                   pl.BlockSpec((None, None, d, tf), lambda b, t, j: (l, i, 0, j)),
                   pl.BlockSpec((None, None, tf, d), lambda b, t, j: (l, i, j, 0))]
        out_specs += [up_spec, up_spec, down_spec]
        out_shape += [jax.ShapeDtypeStruct((d, ff), BF16), jax.ShapeDtypeStruct((d, ff), BF16),
                      jax.ShapeDtypeStruct((ff, d), BF16)]
    else:
        w_specs = [up_spec, up_spec, down_spec]
    out = pl.pallas_call(
        functools.partial(_ffn_kernel, final_norm=final_norm, norm_rows=norm_rows,
                          emit_bf16_weights=(kind == "f32")),
        grid=grid,
        in_specs=[
            pl.BlockSpec(memory_space=pl.ANY),
            *mspec,
            pl.BlockSpec((1, d), lambda b, t, j: (0, 0)),
            *w_specs,
            pl.BlockSpec((1, d), lambda b, t, j: (0, 0)),
        ],
        out_specs=out_specs,
        out_shape=out_shape,
        scratch_shapes=[pltpu.VMEM((bt, tt, d), F32),
                        pltpu.VMEM((bt * tt, d), BF16),
                        pltpu.SemaphoreType.DMA(())],
        compiler_params=pltpu.CompilerParams(
            dimension_semantics=("arbitrary", "arbitrary", "arbitrary"),
            vmem_limit_bytes=V7X_VMEM_LIMIT_BYTES),
        name="ffn_half_step",
    )(h, mod, mod, mod, norm_g, wg, wu, wd, final_g)
    return out[0], (tuple(out[1:]) if kind == "f32" else (wg, wu, wd))


def _mod_index(b, *_, row, bt):
    return (row // bt + b, 0, 0)


def _mixer_ab_kernel(h_ref, shift_ref, scale_ref, gate_ref, ng_ref, hist_ref, win_ref, vg_ref,
                     vb_ref, ws_ref, bst_ref, dw_ref, lng_ref, lnb_ref, wout_ref,
                     o_ref, convnew_ref, *rest, emit_vn, conv_rows, conv_cols, unroll_conv):
    if emit_vn:
        vn_ref, xpad_ref, aout_ref, conv_ref, xsh_ref, dwb_ref = rest
    else:
        xpad_ref, aout_ref, conv_ref, xsh_ref, dwb_ref = rest
    t = pl.program_id(1)
    bt, tt, d = h_ref.shape
    aw = vg_ref.shape[-1]
    bw = lng_ref.shape[-1]
    n_grp, a_chunk, _ = ws_ref.shape
    gd = aw // n_grp
    n_taps = dw_ref.shape[0]
    hpad = xpad_ref.shape[1] - tt
    off = hpad - (n_taps - 1)
    cr = min(a_chunk, tt)

    @pl.when(t == 0)
    def _():
        xpad_ref[:, 0:hpad, :] = hist_ref[...]
        for k in range(n_taps):
            dwb_ref[k] = jnp.broadcast_to(dw_ref[k:k + 1, :], (V7X_SUBLANES, bw))

    h = h_ref[...]
    hn = _rms_mod(h, ng_ref[...], shift_ref[...], scale_ref[...]).astype(BF16).reshape(bt * tt, d)

    ba = _dot(hn, win_ref[:, 2 * aw:2 * aw + bw])
    bg = _dot(hn, win_ref[:, 2 * aw + bw:2 * aw + 2 * bw])
    xpad_ref[:, hpad:hpad + tt, :] = (ba * _sigmoid(bg)).reshape(bt, tt, bw)

    n_sh = xsh_ref.shape[1]

    def conv_block(s, r0, cols):
        tiles = conv_rows // V7X_SUBLANES
        acc = jnp.zeros((tiles, V7X_SUBLANES, conv_cols), F32)
        for k in range(n_taps):
            r = (off + k) % V7X_SUBLANES
            a = (off + k) - r
            if r == 0:
                x = xpad_ref[s, pl.ds(r0 + a, conv_rows), cols]
            else:
                x = xsh_ref[r - 1, pl.ds(r0 + a, conv_rows), cols]
            acc = acc + x.reshape(tiles, V7X_SUBLANES, conv_cols) * dwb_ref[k, :, cols]
        conv_ref[pl.ds(s * tt + r0, conv_rows), cols] = acc.reshape(conv_rows, conv_cols)

    for s in range(bt):
        for r in range(1, V7X_SUBLANES):
            xsh_ref[r - 1] = xpad_ref[s, r:r + n_sh, :]
        for c0 in range(0, bw, conv_cols):
            cols = slice(c0, c0 + conv_cols)
            if unroll_conv:
                for r0 in range(0, tt, conv_rows):
                    conv_block(s, r0, cols)
            else:
                def conv_body(rb, carry, s=s, cols=cols):
                    conv_block(s, pl.multiple_of(rb * conv_rows, conv_rows), cols)
                    return carry

                lax.fori_loop(0, tt // conv_rows, conv_body, 0)

    u = jax.nn.gelu(_dot(hn, win_ref[:, 0:aw]))
    v = jax.nn.gelu(_dot(hn, win_ref[:, aw:2 * aw]))
    vn = _layer_norm(v, vg_ref[...], vb_ref[...])
    if emit_vn:
        vn_ref[...] = vn.reshape(bt, tt, aw)
    vnb = vn.astype(BF16)
    gate_rows = min(V7X_MXU_DIM, bt * tt)
    row = lax.broadcasted_iota(jnp.int32, (a_chunk, a_chunk), 0)
    col = lax.broadcasted_iota(jnp.int32, (a_chunk, a_chunk), 1)
    keep = (row < cr) & (col < cr) & (col // STREAM_CHUNK <= row // STREAM_CHUNK)
    for g in range(n_grp):
        corner = jnp.where(keep, ws_ref[g], 0.0)
        diag = corner
        for i in range(1, a_chunk // cr):
            diag = diag + pltpu.roll(pltpu.roll(corner, i * cr, 0), i * cr, 1)
        diag = diag.astype(BF16)
        nb = gate_rows // a_chunk
        zero = jnp.zeros_like(diag)
        wbd = diag if nb <= 1 else jnp.concatenate(
            [jnp.concatenate([diag if i == j else zero for j in range(nb)], axis=1)
             for i in range(nb)], axis=0)
        wbd = wbd[0:gate_rows, 0:gate_rows]
        bias = jnp.concatenate([bst_ref[0:cr, g:g + 1]] * (gate_rows // cr), axis=0)
        for r0 in range(0, bt * tt, gate_rows):
            gate = _dot(wbd, vnb[r0:r0 + gate_rows, g * gd:(g + 1) * gd]) + bias
            aout_ref[r0:r0 + gate_rows, g * gd:(g + 1) * gd] = (
                u[r0:r0 + gate_rows, g * gd:(g + 1) * gd] * gate).astype(BF16)

    conv = conv_ref[...]
    bout = _silu(_layer_norm(conv, lng_ref[...], lnb_ref[...])).astype(BF16)

    y = _dot(aout_ref[...], wout_ref[0:aw, :]) + _dot(bout, wout_ref[aw:aw + bw, :])
    o_ref[...] = h + gate_ref[...] * y.reshape(bt, tt, d)

    tail = xpad_ref[:, tt:tt + hpad, :]
    xpad_ref[:, 0:hpad, :] = tail

    @pl.when(t == pl.num_programs(1) - 1)
    def _():
        convnew_ref[...] = tail


def _mixer_ab_call(h, mod, mod_row, norm_g, hist, w_in, v_g, v_b, ws, bs_t, dw, ln_g, ln_b, w_out,
                   *, bt, tt, emit_vn):
    bsz, seq, d = h.shape
    aw, bw = v_g.shape[-1], ln_g.shape[-1]
    hpad = hist.shape[1]
    conv_rows = min(tt, 32)
    conv_cols = 256
    unroll_conv = bt * (bw // conv_cols) * (tt // conv_rows) <= MAX_UNROLLED_CONV_BLOCKS
    once = dict(pipeline_mode=pl.Buffered(1))
    full2 = lambda a: pl.BlockSpec(a.shape, lambda b, t: (0, 0))
    mspec = [pl.BlockSpec((bt, 1, d), functools.partial(_mod_index, row=mod_row(k), bt=bt))
             for k in range(3)]
    out_specs = [pl.BlockSpec((bt, tt, d), lambda b, t: (b, t, 0)),
                 pl.BlockSpec((bt, hpad, bw), lambda b, t: (b, 0, 0))]
    out_shape = [jax.ShapeDtypeStruct(h.shape, F32),
                 jax.ShapeDtypeStruct((bsz, hpad, bw), F32)]
    if emit_vn:
        out_specs.append(pl.BlockSpec((bt, tt, aw), lambda b, t: (b, t, 0)))
        out_shape.append(jax.ShapeDtypeStruct((bsz, seq, aw), F32))
    return pl.pallas_call(
        functools.partial(_mixer_ab_kernel, emit_vn=emit_vn, conv_rows=conv_rows,
                          conv_cols=conv_cols, unroll_conv=unroll_conv),
        grid=(bsz // bt, seq // tt),
        in_specs=[
            pl.BlockSpec((bt, tt, d), lambda b, t: (b, t, 0)),
            *mspec,
            full2(norm_g),
            pl.BlockSpec((bt, hpad, bw), lambda b, t: (b, 0, 0)),
            pl.BlockSpec(w_in.shape, lambda b, t: (0, 0), **once),
            full2(v_g), full2(v_b),
            pl.BlockSpec(ws.shape, lambda b, t: (0, 0, 0)),
            full2(bs_t), full2(dw), full2(ln_g), full2(ln_b),
            pl.BlockSpec(w_out.shape, lambda b, t: (0, 0), **once),
        ],
        out_specs=out_specs,
        out_shape=out_shape,
        scratch_shapes=[pltpu.VMEM((bt, hpad + tt, bw), F32),
                        pltpu.VMEM((bt * tt, aw), BF16),
                        pltpu.VMEM((bt * tt, bw), F32),
                        pltpu.VMEM((V7X_SUBLANES - 1, hpad + tt - V7X_SUBLANES, bw), F32),
                        pltpu.VMEM((dw.shape[0], V7X_SUBLANES, bw), F32)],
        compiler_params=pltpu.CompilerParams(
            dimension_semantics=("arbitrary", "arbitrary"),
            vmem_limit_bytes=V7X_VMEM_LIMIT_BYTES),
        name="mixer_ab",
    )(h, mod, mod, mod, norm_g, hist, w_in, v_g, v_b, ws, bs_t, dw, ln_g, ln_b, w_out)


def _mixer_c_kernel(h_ref, shift_ref, scale_ref, gate_ref, ng_ref, hist_ref, wgrp_ref, bgrp_ref,
                    cs_ref, o_ref, poolnew_ref, xpad_ref, y_ref, lvl_ref, *, start_pos):
    t = pl.program_id(1)
    bt, tt, d = h_ref.shape
    n_grp = wgrp_ref.shape[0]
    gd = d // n_grp
    hpad = xpad_ref.shape[1] - tt

    @pl.when(t == 0)
    def _():
        xpad_ref[:, 0:hpad, :] = hist_ref[...]

    h = h_ref[...]
    hn = _rms_mod(h, ng_ref[...], shift_ref[...], scale_ref[...])
    xpad_ref[:, hpad:hpad + tt, :] = hn

    end = hpad + tt

    def window_sum(cols, w):
        m = w.bit_length() - 1
        assert w == 1 << m and hpad - w + 1 >= 0

        def read(level, row0, n):
            if level == 0:
                return xpad_ref[:, row0:row0 + n, cols]
            return lvl_ref[(level - 1) % 2, :, row0:row0 + n, :]

        for i in range(1, m):
            k = 1 << (i - 1)
            lo = (hpad - w + (1 << i)) // V7X_SUBLANES * V7X_SUBLANES
            buf = (i - 1) % 2
            if lo > 0:
                lvl_ref[buf, :, 0:lo, :] = jnp.zeros((bt, lo, gd), F32)
            lvl_ref[buf, :, lo:end, :] = read(i - 1, lo, end - lo) + read(i - 1, lo - k, end - lo)
        k = 1 << (m - 1)
        return read(m - 1, hpad, tt) + read(m - 1, hpad - k, tt)

    pos = start_pos + t * tt + lax.broadcasted_iota(jnp.int32, (1, tt, 1), 1)
    for g, w in enumerate(POOL_WINDOWS):
        cols = slice(g * gd, (g + 1) * gd)
        s = window_sum(cols, w)
        cnt = jnp.minimum(w, pos + 1).astype(F32)
        pooled = (s / cnt - hn[:, :, cols]).astype(BF16).reshape(bt * tt, gd)
        y_ref[:, cols] = _dot(pooled, wgrp_ref[g])
    y = (y_ref[...] + bgrp_ref[...]) * cs_ref[...]
    o_ref[...] = h + gate_ref[...] * y.reshape(bt, tt, d)

    tail = xpad_ref[:, tt:tt + hpad, :]
    xpad_ref[:, 0:hpad, :] = tail

    @pl.when(t == pl.num_programs(1) - 1)
    def _():
        poolnew_ref[...] = tail


def _mixer_c_call(h, mod, mod_row, norm_g, hist, w_grp, b_grp, c_scale, *, bt, tt, start_pos):
    bsz, seq, d = h.shape
    hpad = hist.shape[1]
    full2 = lambda a: pl.BlockSpec(a.shape, lambda b, t: (0, 0))
    mspec = [pl.BlockSpec((bt, 1, d), functools.partial(_mod_index, row=mod_row(k), bt=bt))
             for k in range(3)]
    return pl.pallas_call(
        functools.partial(_mixer_c_kernel, start_pos=start_pos),
        grid=(bsz // bt, seq // tt),
        in_specs=[
            pl.BlockSpec((bt, tt, d), lambda b, t: (b, t, 0)),
            *mspec,
            full2(norm_g),
            pl.BlockSpec((bt, hpad, d), lambda b, t: (b, 0, 0)),
            pl.BlockSpec(w_grp.shape, lambda b, t: (0, 0, 0)),
            full2(b_grp), full2(c_scale),
        ],
        out_specs=[pl.BlockSpec((bt, tt, d), lambda b, t: (b, t, 0)),
                   pl.BlockSpec((bt, hpad, d), lambda b, t: (b, 0, 0))],
        out_shape=[jax.ShapeDtypeStruct(h.shape, F32),
                   jax.ShapeDtypeStruct((bsz, hpad, d), F32)],
        scratch_shapes=[pltpu.VMEM((bt, hpad + tt, d), F32),
                        pltpu.VMEM((bt * tt, d), F32),
                        pltpu.VMEM((2, bt, hpad + tt, d // w_grp.shape[0]), F32)],
        compiler_params=pltpu.CompilerParams(
            dimension_semantics=("arbitrary", "arbitrary"),
            vmem_limit_bytes=V7X_VMEM_LIMIT_BYTES),
        name="mixer_c",
    )(h, mod, mod, mod, norm_g, hist, w_grp, b_grp, c_scale)


def _front_pad(hist, rows):
    return jnp.pad(hist, ((0, 0), (0, 0), (rows - hist.shape[2], 0), (0, 0)))


def _run_trunk(x, mod, row0, start_pos, conv_hist, pool_hist, p, ffn_weights, *, bt, tt_ffn,
               tt_mix, tf, emit_vn):
    n_layers = p["norm_g"].shape[0]
    n_rows = mod.shape[2]
    d = x.shape[-1]
    mod_tab = mod.reshape(n_layers * 9 * n_rows, 1, d)
    n_conv = conv_hist.shape[2]
    n_pool = pool_hist.shape[2]
    conv_hist = _front_pad(conv_hist, _round_up(n_conv, V7X_SUBLANES))
    pool_hist = _front_pad(pool_hist, _round_up(n_pool, V7X_SUBLANES) + V7X_SUBLANES)

    h = x
    new_conv, new_pool, new_av, bf16_weights = [], [], [], []
    for l in range(n_layers):
        def mod_row(sub, l=l):
            return lambda k: ((l * 9 + sub * 3 + k) * n_rows + row0)

        ffn = functools.partial(_ffn_call, final_g=p["final_norm_g"], bt=bt, tt=tt_ffn, tf=tf)
        h, w_first = ffn(h, mod_tab, mod_row(0), p["norm_g"][l, 0:1], ffn_weights[l][0],
                         final_norm=False)
        if l % 2 == 0:
            e = l // 2
            out = _mixer_ab_call(
                h, mod_tab, mod_row(1), p["norm_g"][l, 1:2], conv_hist[e], p["ab_w_in"][e],
                p["a_v_norm_g"][e:e + 1], p["a_v_norm_b"][e:e + 1], p["a_ws"][e], p["a_bs"][e].T,
                p["b_dw"][e], p["b_ln_g"][e:e + 1], p["b_ln_b"][e:e + 1], p["ab_w_out"][e],
                bt=bt, tt=tt_mix, emit_vn=emit_vn)
            h = out[0]
            new_conv.append(out[1][:, -n_conv:])
            if emit_vn:
                new_av.append(out[2])
        else:
            o = l // 2
            h, ph = _mixer_c_call(
                h, mod_tab, mod_row(1), p["norm_g"][l, 1:2], pool_hist[o], p["c_w_grp"][o],
                p["c_b_grp"][o].reshape(1, d), p["c_scale"][o:o + 1],
                bt=bt, tt=tt_mix, start_pos=start_pos)
            new_pool.append(ph[:, -n_pool:])
        h, w_second = ffn(h, mod_tab, mod_row(2), p["norm_g"][l, 2:3], ffn_weights[l][1],
                          final_norm=(l == n_layers - 1))
        bf16_weights.append((("bf16", w_first), ("bf16", w_second)))
    outs = (h, jnp.stack(new_conv), jnp.stack(new_pool), jnp.stack(new_av) if emit_vn else None)
    return outs, bf16_weights


def kernel(x_prompt, x_sample, c_prompt, c_sample, state_conv_b, state_pool_c, ada_w, ada_b, norm_g, final_norm_g, ffn_w_gate, ffn_w_up, ffn_w_down, ab_w_in, a_v_norm_g, a_v_norm_b, a_ws, a_bs, b_dw, b_ln_g, b_ln_b, ab_w_out, c_w_grp, c_b_grp, c_scale):
    bp, sp, d = x_prompt.shape
    bs, ss, _ = x_sample.shape
    n_layers = norm_g.shape[0]
    p = {
        "norm_g": norm_g, "final_norm_g": final_norm_g.reshape(1, d),
        "ab_w_in": ab_w_in.astype(BF16), "ab_w_out": ab_w_out.astype(BF16),
        "a_v_norm_g": a_v_norm_g, "a_v_norm_b": a_v_norm_b, "a_ws": a_ws, "a_bs": a_bs,
        "b_dw": b_dw, "b_ln_g": b_ln_g, "b_ln_b": b_ln_b,
        "c_w_grp": c_w_grp.astype(BF16), "c_b_grp": c_b_grp, "c_scale": c_scale,
    }
    n_rows = _round_up(bs + bp, V7X_SUBLANES)
    c_all = jnp.concatenate([c_sample, c_prompt, jnp.zeros((n_rows - bs - bp, d), F32)], axis=0)
    mod = _ada_call(c_all, ada_w, ada_b)

    f32_weights = [[("f32", (ffn_w_gate, ffn_w_up, ffn_w_down), l, i) for i in range(2)]
                   for l in range(n_layers)]
    (y_s, conv_s, pool_s, av_s), bf16_weights = _run_trunk(
        x_sample, mod, 0, PAST_LEN, state_conv_b, state_pool_c, p, f32_weights,
        bt=bs, tt_ffn=ss, tt_mix=ss, tf=SAMPLE_FFN_HIDDEN_TILE, emit_vn=True)

    n_even, n_odd = state_conv_b.shape[0], state_pool_c.shape[0]
    conv0 = jnp.zeros((n_even, bp) + state_conv_b.shape[2:], F32)
    pool0 = jnp.zeros((n_odd, bp) + state_pool_c.shape[2:], F32)
    (y_p, conv_p, pool_p, _), _ = _run_trunk(
        x_prompt, mod, bs, 0, conv0, pool0, p, bf16_weights,
        bt=1, tt_ffn=min(sp, PROMPT_FFN_ROWS), tt_mix=min(sp, PROMPT_MIXER_ROWS),
        tf=FFN_HIDDEN_TILE, emit_vn=False)
    return (y_p, y_s, conv_p, conv_s, pool_p, pool_s, av_s)
```

```python
import functools

import jax
import jax.numpy as jnp
from jax import lax
from jax.experimental import pallas as pl
from jax.experimental.pallas import tpu as pltpu

EPS = 1e-6
STREAM_CHUNK = 64
POOL_WINDOWS = (2, 4, 8, 16)
PAST_LEN = 1024

V7X_SUBLANES = 8
V7X_MXU_DIM = 256
V7X_VMEM_LIMIT_BYTES = 56 * 1024 * 1024

PROMPT_FFN_ROWS = 1024
PROMPT_MIXER_ROWS = 256
FFN_HIDDEN_TILE = 512
SAMPLE_FFN_HIDDEN_TILE = 256
FFN_NORM_SPREAD_STEPS = 8
MAX_UNROLLED_CONV_BLOCKS = 32

F32 = jnp.float32
BF16 = jnp.bfloat16


def _round_up(n, m):
    return (n + m - 1) // m * m


def _rms_mod(h, g, shift, scale):
    y = h * lax.rsqrt(jnp.mean(h * h, axis=-1, keepdims=True) + EPS) * g
    return y * (1 + scale) + shift


def _layer_norm(x, g, b):
    mu = jnp.mean(x, axis=-1, keepdims=True)
    d = x - mu
    var = jnp.mean(d * d, axis=-1, keepdims=True)
    return d * lax.rsqrt(var + EPS) * g + b


def _dot(a, b):
    return jnp.dot(a, b, preferred_element_type=F32)


def _sigmoid(x):
    return 0.5 * jnp.tanh(0.5 * x) + 0.5


def _silu(x):
    return x * _sigmoid(x)


def _ada_kernel(c_ref, w_ref, b_ref, o_ref):
    sc = _silu(c_ref[...]).astype(BF16)
    o_ref[...] = _dot(sc, w_ref[...].astype(BF16)) + b_ref[...]


def _ada_call(c_all, ada_w, ada_b):
    n_layers, d, nd = ada_w.shape
    n_comp = nd // d
    rows = c_all.shape[0]
    tn = d // 2
    per = d // tn
    return pl.pallas_call(
        _ada_kernel,
        grid=(n_layers, nd // tn),
        in_specs=[
            pl.BlockSpec((rows, d), lambda l, n: (0, 0)),
            pl.BlockSpec((None, d, tn), lambda l, n: (l, 0, n)),
            pl.BlockSpec((None, 1, tn), lambda l, n: (l, 0, n)),
        ],
        out_specs=pl.BlockSpec((None, None, rows, tn), lambda l, n: (l, n // per, 0, n % per)),
        out_shape=jax.ShapeDtypeStruct((n_layers, n_comp, rows, d), F32),
        compiler_params=pltpu.CompilerParams(
            dimension_semantics=("arbitrary", "arbitrary"),
            vmem_limit_bytes=V7X_VMEM_LIMIT_BYTES),
        name="ada_proj",
    )(c_all, ada_w, ada_b.reshape(n_layers, 1, nd))


def _ffn_kernel(h_hbm, shift_ref, scale_ref, gate_ref, nshift_ref, nscale_ref, ng_ref, wg_ref, wu_ref,
                wd_ref, fg_ref, o_ref, *rest, final_norm, norm_rows, emit_bf16_weights, spread_steps):
    if emit_bf16_weights:
        wg_out, wu_out, wd_out, *rest = rest
    hbuf_ref, *hn_refs, sem = rest
    b, t, j = pl.program_id(0), pl.program_id(1), pl.program_id(2)
    nb, nt, nj = pl.num_programs(0), pl.num_programs(1), pl.num_programs(2)
    bt, tt, d = o_ref.shape
    spread = spread_steps > 0
    tile = b * nt + t
    is_first, is_last = tile == 0, tile == nb * nt - 1
    not_last = jnp.logical_not(is_last)
    last_t = t == nt - 1
    next_b, next_t = jnp.where(last_t, b + 1, b), jnp.where(last_t, 0, t + 1)

    def h_copy(bb, ti):
        return pltpu.make_async_copy(
            h_hbm.at[pl.ds(bb * bt, bt), pl.ds(ti * tt, tt), :], hbuf_ref, sem)

    def norm_chunk(dst_ref, s, r, gmul, shift, init_acc):
        x = hbuf_ref[s, pl.ds(r, norm_rows), :]
        inv = lax.rsqrt(jnp.mean(x * x, axis=-1, keepdims=True) + EPS)
        y = ((x * inv) * gmul + shift).astype(BF16)
        dst_ref[pl.ds(s * tt + r, norm_rows), :] = y
        if init_acc:
            o_ref[s, pl.ds(r, norm_rows), :] = x
        return y

    def whole_tile_prologue():
        for s in range(bt):
            gmul = ng_ref[...] * (1 + scale_ref[s])
            shift = shift_ref[s]

            def rows(i, carry, s=s, gmul=gmul, shift=shift):
                norm_chunk(hn_refs[0], s, pl.multiple_of(i * norm_rows, norm_rows), gmul, shift, True)
                return carry

            lax.fori_loop(0, tt // norm_rows, rows, 0, unroll=4)

    @pl.when(j == 0)
    def _():
        @pl.when(is_first)
        def _():
            h_copy(b, t).start()

        if spread:
            @pl.when(is_first)
            def _():
                h_copy(b, t).wait()
                whole_tile_prologue()

            @pl.when(jnp.logical_not(is_first))
            def _():
                o_ref[...] = hbuf_ref[...]
        else:
            h_copy(b, t).wait()
            whole_tile_prologue()

    @pl.when((j == jnp.minimum(1, nj - 1)) & not_last)
    def _():
        h_copy(next_b, next_t).start()

    if spread:
        @pl.when((j == 2) & not_last)
        def _():
            h_copy(next_b, next_t).wait()

    def step(norm_next, parity):
        if emit_bf16_weights:
            wg, wu, wd = (w[...].astype(BF16) for w in (wg_ref, wu_ref, wd_ref))
            wg_out[...], wu_out[...], wd_out[...] = wg, wu, wd
        else:
            wg, wu, wd = wg_ref[...], wu_ref[...], wd_ref[...]
        hn = hn_refs[parity][...]
        gated = _dot(hn, wg)
        if norm_next:
            rows_per_step = tt // spread_steps
            n_sub = rows_per_step // norm_rows
            blk = bt * tt // n_sub
            gmul = ng_ref[...] * (1 + nscale_ref[0])
            shift = nshift_ref[0]
            pieces = []
            for k in range(n_sub):
                r = pl.multiple_of((j - 2) * rows_per_step + k * norm_rows, norm_rows)
                y = norm_chunk(hn_refs[1 - parity], 0, r, gmul, shift, False)
                bits = lax.bitcast_convert_type(y[:, 0:gated.shape[1]].astype(F32), jnp.int32)
                zero = lax.shift_right_logical(lax.shift_right_logical(bits, 16), 16).astype(F32)
                pieces += [gated[k * blk:k * blk + norm_rows] + zero,
                           gated[k * blk + norm_rows:(k + 1) * blk]]
            gated = jnp.concatenate(pieces, axis=0)
        act = (_silu(gated) * _dot(hn, wu)).astype(BF16)
        o_ref[...] += (0.5 * gate_ref[...]) * _dot(act, wd).reshape(bt, tt, d)

    if spread:
        in_window = (j >= 2) & (j < 2 + spread_steps) & not_last
        for parity in range(2):
            mine = tile % 2 == parity
            pl.when(mine & in_window)(functools.partial(step, True, parity))
            pl.when(mine & jnp.logical_not(in_window))(functools.partial(step, False, parity))
    else:
        step(False, 0)

    if final_norm:
        @pl.when(j == nj - 1)
        def _():
            y = o_ref[...]
            o_ref[...] = (y * lax.rsqrt(jnp.mean(y * y, axis=-1, keepdims=True) + EPS)) * fg_ref[...]


def _ffn_call(h, mod, mod_row, norm_g, weights, final_g, *, bt, tt, tf, final_norm):
    bsz, seq, d = h.shape
    kind, (wg, wu, wd) = weights[0], weights[1]
    ff = wg.shape[-1]
    grid = (bsz // bt, seq // tt, ff // tf)
    mspec = [pl.BlockSpec((bt, 1, d), functools.partial(_mod_index, row=mod_row(k), bt=bt))
             for k in range(3)]
    norm_rows = 16
    assert seq % tt == 0 and bsz % bt == 0 and ff % tf == 0 and tt % (4 * norm_rows) == 0
    spread_steps = FFN_NORM_SPREAD_STEPS
    if not (bt == 1 and grid[0] * grid[1] > 1 and grid[2] >= spread_steps + 2
            and tt % (spread_steps * norm_rows) == 0):
        spread_steps = 0
    next_index = lambda b, t, j: (jnp.where(t == grid[1] - 1, jnp.minimum(b + 1, grid[0] - 1), b),)
    nspec = [pl.BlockSpec((bt, 1, d), lambda b, t, j, k=k: _mod_index(
        *next_index(b, t, j), row=mod_row(k), bt=bt)) for k in range(2)]
    up_spec = pl.BlockSpec((d, tf), lambda b, t, j: (0, j))
    down_spec = pl.BlockSpec((tf, d), lambda b, t, j: (j, 0))
    out_specs = [pl.BlockSpec((bt, tt, d), lambda b, t, j: (b, t, 0))]
    out_shape = [jax.ShapeDtypeStruct(h.shape, F32)]
    if kind == "f32":
        l, i = weights[2], weights[3]
        assert grid[:2] == (1, 1)
        w_specs = [pl.BlockSpec((None, None, d, tf), lambda b, t, j: (l, i, 0, j)),
                   pl.BlockSpec((None, None, d, tf), lambda b, t, j: (l, i, 0, j)),
                   pl.BlockSpec((None, None, tf, d), lambda b, t, j: (l, i, j, 0))]
        out_specs += [up_spec, up_spec, down_spec]
        out_shape += [jax.ShapeDtypeStruct((d, ff), BF16), jax.ShapeDtypeStruct((d, ff), BF16),
                      jax.ShapeDtypeStruct((ff, d), BF16)]
    else:
        w_specs = [up_spec, up_spec, down_spec]
    out = pl.pallas_call(
        functools.partial(_ffn_kernel, final_norm=final_norm, norm_rows=norm_rows,
                          emit_bf16_weights=(kind == "f32"), spread_steps=spread_steps),
        grid=grid,
        in_specs=[
            pl.BlockSpec(memory_space=pl.ANY),
            *mspec,
            *nspec,
            pl.BlockSpec((1, d), lambda b, t, j: (0, 0)),
            *w_specs,
            pl.BlockSpec((1, d), lambda b, t, j: (0, 0)),
        ],
        out_specs=out_specs,
        out_shape=out_shape,
        scratch_shapes=[pltpu.VMEM((bt, tt, d), F32),
                        *[pltpu.VMEM((bt * tt, d), BF16)] * (2 if spread_steps else 1),
                        pltpu.SemaphoreType.DMA(())],
        compiler_params=pltpu.CompilerParams(
            dimension_semantics=("arbitrary", "arbitrary", "arbitrary"),
            vmem_limit_bytes=V7X_VMEM_LIMIT_BYTES),
        name="ffn_half_step",
    )(h, mod, mod, mod, mod, mod, norm_g, wg, wu, wd, final_g)
    return out[0], (tuple(out[1:]) if kind == "f32" else (wg, wu, wd))


def _mod_index(b, *_, row, bt):
    return (row // bt + b, 0, 0)


def _mixer_ab_kernel(h_ref, shift_ref, scale_ref, gate_ref, ng_ref, hist_ref, win_ref, vg_ref,
                     vb_ref, ws_ref, bst_ref, dw_ref, lng_ref, lnb_ref, wout_ref,
                     o_ref, convnew_ref, *rest, emit_vn, conv_rows, conv_cols, unroll_conv):
    if emit_vn:
        vn_ref, xpad_ref, aout_ref, conv_ref, xsh_ref, dwb_ref = rest
    else:
        xpad_ref, aout_ref, conv_ref, xsh_ref, dwb_ref = rest
    t = pl.program_id(1)
    bt, tt, d = h_ref.shape
    aw = vg_ref.shape[-1]
    bw = lng_ref.shape[-1]
    n_grp, a_chunk, _ = ws_ref.shape
    gd = aw // n_grp
    n_taps = dw_ref.shape[0]
    hpad = xpad_ref.shape[1] - tt
    off = hpad - (n_taps - 1)
    cr = min(a_chunk, tt)

    @pl.when(t == 0)
    def _():
        xpad_ref[:, 0:hpad, :] = hist_ref[...]
        for k in range(n_taps):
            dwb_ref[k] = jnp.broadcast_to(dw_ref[k:k + 1, :], (V7X_SUBLANES, bw))

    h = h_ref[...]
    hn = _rms_mod(h, ng_ref[...], shift_ref[...], scale_ref[...]).astype(BF16).reshape(bt * tt, d)

    ba = _dot(hn, win_ref[:, 2 * aw:2 * aw + bw])
    bg = _dot(hn, win_ref[:, 2 * aw + bw:2 * aw + 2 * bw])
    xpad_ref[:, hpad:hpad + tt, :] = (ba * _sigmoid(bg)).reshape(bt, tt, bw)

    n_sh = xsh_ref.shape[1]

    def conv_block(s, r0, cols):
        tiles = conv_rows // V7X_SUBLANES
        acc = jnp.zeros((tiles, V7X_SUBLANES, conv_cols), F32)
        for k in range(n_taps):
            r = (off + k) % V7X_SUBLANES
            a = (off + k) - r
            if r == 0:
                x = xpad_ref[s, pl.ds(r0 + a, conv_rows), cols]
            else:
                x = xsh_ref[r - 1, pl.ds(r0 + a, conv_rows), cols]
            acc = acc + x.reshape(tiles, V7X_SUBLANES, conv_cols) * dwb_ref[k, :, cols]
        conv_ref[pl.ds(s * tt + r0, conv_rows), cols] = acc.reshape(conv_rows, conv_cols)

    for s in range(bt):
        for r in range(1, V7X_SUBLANES):
            xsh_ref[r - 1] = xpad_ref[s, r:r + n_sh, :]
        for c0 in range(0, bw, conv_cols):
            cols = slice(c0, c0 + conv_cols)
            if unroll_conv:
                for r0 in range(0, tt, conv_rows):
                    conv_block(s, r0, cols)
            else:
                def conv_body(rb, carry, s=s, cols=cols):
                    conv_block(s, pl.multiple_of(rb * conv_rows, conv_rows), cols)
                    return carry

                lax.fori_loop(0, tt // conv_rows, conv_body, 0)

    u = jax.nn.gelu(_dot(hn, win_ref[:, 0:aw]))
    v = jax.nn.gelu(_dot(hn, win_ref[:, aw:2 * aw]))
    vn = _layer_norm(v, vg_ref[...], vb_ref[...])
    if emit_vn:
        vn_ref[...] = vn.reshape(bt, tt, aw)
    vnb = vn.astype(BF16)
    gate_rows = min(V7X_MXU_DIM, bt * tt)
    row = lax.broadcasted_iota(jnp.int32, (a_chunk, a_chunk), 0)
    col = lax.broadcasted_iota(jnp.int32, (a_chunk, a_chunk), 1)
    keep = (row < cr) & (col < cr) & (col // STREAM_CHUNK <= row // STREAM_CHUNK)
    for g in range(n_grp):
        corner = jnp.where(keep, ws_ref[g], 0.0)
        diag = corner
        for i in range(1, a_chunk // cr):
            diag = diag + pltpu.roll(pltpu.roll(corner, i * cr, 0), i * cr, 1)
        diag = diag.astype(BF16)
        nb = gate_rows // a_chunk
        zero = jnp.zeros_like(diag)
        wbd = diag if nb <= 1 else jnp.concatenate(
            [jnp.concatenate([diag if i == j else zero for j in range(nb)], axis=1)
             for i in range(nb)], axis=0)
        wbd = wbd[0:gate_rows, 0:gate_rows]
        bias = jnp.concatenate([bst_ref[0:cr, g:g + 1]] * (gate_rows // cr), axis=0)
        for r0 in range(0, bt * tt, gate_rows):
            gate = _dot(wbd, vnb[r0:r0 + gate_rows, g * gd:(g + 1) * gd]) + bias
            aout_ref[r0:r0 + gate_rows, g * gd:(g + 1) * gd] = (
                u[r0:r0 + gate_rows, g * gd:(g + 1) * gd] * gate).astype(BF16)

    conv = conv_ref[...]
    bout = _silu(_layer_norm(conv, lng_ref[...], lnb_ref[...])).astype(BF16)

    y = _dot(aout_ref[...], wout_ref[0:aw, :]) + _dot(bout, wout_ref[aw:aw + bw, :])
    o_ref[...] = h + gate_ref[...] * y.reshape(bt, tt, d)

    tail = xpad_ref[:, tt:tt + hpad, :]
    xpad_ref[:, 0:hpad, :] = tail

    @pl.when(t == pl.num_programs(1) - 1)
    def _():
        convnew_ref[...] = tail


def _mixer_ab_call(h, mod, mod_row, norm_g, hist, w_in, v_g, v_b, ws, bs_t, dw, ln_g, ln_b, w_out,
                   *, bt, tt, emit_vn):
    bsz, seq, d = h.shape
    aw, bw = v_g.shape[-1], ln_g.shape[-1]
    hpad = hist.shape[1]
    conv_rows = min(tt, 32)
    conv_cols = 256
    unroll_conv = bt * (bw // conv_cols) * (tt // conv_rows) <= MAX_UNROLLED_CONV_BLOCKS
    once = dict(pipeline_mode=pl.Buffered(1))
    full2 = lambda a: pl.BlockSpec(a.shape, lambda b, t: (0, 0))
    mspec = [pl.BlockSpec((bt, 1, d), functools.partial(_mod_index, row=mod_row(k), bt=bt))
             for k in range(3)]
    out_specs = [pl.BlockSpec((bt, tt, d), lambda b, t: (b, t, 0)),
                 pl.BlockSpec((bt, hpad, bw), lambda b, t: (b, 0, 0))]
    out_shape = [jax.ShapeDtypeStruct(h.shape, F32),
                 jax.ShapeDtypeStruct((bsz, hpad, bw), F32)]
    if emit_vn:
        out_specs.append(pl.BlockSpec((bt, tt, aw), lambda b, t: (b, t, 0)))
        out_shape.append(jax.ShapeDtypeStruct((bsz, seq, aw), F32))
    return pl.pallas_call(
        functools.partial(_mixer_ab_kernel, emit_vn=emit_vn, conv_rows=conv_rows,
                          conv_cols=conv_cols, unroll_conv=unroll_conv),
        grid=(bsz // bt, seq // tt),
        in_specs=[
            pl.BlockSpec((bt, tt, d), lambda b, t: (b, t, 0)),
            *mspec,
            full2(norm_g),
            pl.BlockSpec((bt, hpad, bw), lambda b, t: (b, 0, 0)),
            pl.BlockSpec(w_in.shape, lambda b, t: (0, 0), **once),
            full2(v_g), full2(v_b),
            pl.BlockSpec(ws.shape, lambda b, t: (0, 0, 0)),
            full2(bs_t), full2(dw), full2(ln_g), full2(ln_b),
            pl.BlockSpec(w_out.shape, lambda b, t: (0, 0), **once),
        ],
        out_specs=out_specs,
        out_shape=out_shape,
        scratch_shapes=[pltpu.VMEM((bt, hpad + tt, bw), F32),
                        pltpu.VMEM((bt * tt, aw), BF16),
                        pltpu.VMEM((bt * tt, bw), F32),
                        pltpu.VMEM((V7X_SUBLANES - 1, hpad + tt - V7X_SUBLANES, bw), F32),
                        pltpu.VMEM((dw.shape[0], V7X_SUBLANES, bw), F32)],
        compiler_params=pltpu.CompilerParams(
            dimension_semantics=("arbitrary", "arbitrary"),
            vmem_limit_bytes=V7X_VMEM_LIMIT_BYTES),
        name="mixer_ab",
    )(h, mod, mod, mod, norm_g, hist, w_in, v_g, v_b, ws, bs_t, dw, ln_g, ln_b, w_out)


def _mixer_c_kernel(h_ref, shift_ref, scale_ref, gate_ref, ng_ref, hist_ref, wgrp_ref, bgrp_ref,
                    cs_ref, o_ref, poolnew_ref, xpad_ref, y_ref, lvl_ref, *, start_pos):
    t = pl.program_id(1)
    bt, tt, d = h_ref.shape
    n_grp = wgrp_ref.shape[0]
    gd = d // n_grp
    hpad = xpad_ref.shape[1] - tt

    @pl.when(t == 0)
    def _():
        xpad_ref[:, 0:hpad, :] = hist_ref[...]

    h = h_ref[...]
    hn = _rms_mod(h, ng_ref[...], shift_ref[...], scale_ref[...])
    xpad_ref[:, hpad:hpad + tt, :] = hn

    end = hpad + tt

    def window_sum(cols, w):
        m = w.bit_length() - 1
        assert w == 1 << m and hpad - w + 1 >= 0

        def read(level, row0, n):
            if level == 0:
                return xpad_ref[:, row0:row0 + n, cols]
            return lvl_ref[(level - 1) % 2, :, row0:row0 + n, :]

        for i in range(1, m):
            k = 1 << (i - 1)
            lo = (hpad - w + (1 << i)) // V7X_SUBLANES * V7X_SUBLANES
            buf = (i - 1) % 2
            if lo > 0:
                lvl_ref[buf, :, 0:lo, :] = jnp.zeros((bt, lo, gd), F32)
            lvl_ref[buf, :, lo:end, :] = read(i - 1, lo, end - lo) + read(i - 1, lo - k, end - lo)
        k = 1 << (m - 1)
        return read(m - 1, hpad, tt) + read(m - 1, hpad - k, tt)

    pos = start_pos + t * tt + lax.broadcasted_iota(jnp.int32, (1, tt, 1), 1)
    for g, w in enumerate(POOL_WINDOWS):
        cols = slice(g * gd, (g + 1) * gd)
        s = window_sum(cols, w)
        cnt = jnp.minimum(w, pos + 1).astype(F32)
        pooled = (s / cnt - hn[:, :, cols]).astype(BF16).reshape(bt * tt, gd)
        y_ref[:, cols] = _dot(pooled, wgrp_ref[g])
    y = (y_ref[...] + bgrp_ref[...]) * cs_ref[...]
    o_ref[...] = h + gate_ref[...] * y.reshape(bt, tt, d)

    tail = xpad_ref[:, tt:tt + hpad, :]
    xpad_ref[:, 0:hpad, :] = tail

    @pl.when(t == pl.num_programs(1) - 1)
    def _():
        poolnew_ref[...] = tail


def _mixer_c_call(h, mod, mod_row, norm_g, hist, w_grp, b_grp, c_scale, *, bt, tt, start_pos):
    bsz, seq, d = h.shape
    hpad = hist.shape[1]
    full2 = lambda a: pl.BlockSpec(a.shape, lambda b, t: (0, 0))
    mspec = [pl.BlockSpec((bt, 1, d), functools.partial(_mod_index, row=mod_row(k), bt=bt))
             for k in range(3)]
    return pl.pallas_call(
        functools.partial(_mixer_c_kernel, start_pos=start_pos),
        grid=(bsz // bt, seq // tt),
        in_specs=[
            pl.BlockSpec((bt, tt, d), lambda b, t: (b, t, 0)),
            *mspec,
            full2(norm_g),
            pl.BlockSpec((bt, hpad, d), lambda b, t: (b, 0, 0)),
            pl.BlockSpec(w_grp.shape, lambda b, t: (0, 0, 0)),
            full2(b_grp), full2(c_scale),
        ],
        out_specs=[pl.BlockSpec((bt, tt, d), lambda b, t: (b, t, 0)),
                   pl.BlockSpec((bt, hpad, d), lambda b, t: (b, 0, 0))],
        out_shape=[jax.ShapeDtypeStruct(h.shape, F32),
                   jax.ShapeDtypeStruct((bsz, hpad, d), F32)],
        scratch_shapes=[pltpu.VMEM((bt, hpad + tt, d), F32),
                        pltpu.VMEM((bt * tt, d), F32),
                        pltpu.VMEM((2, bt, hpad + tt, d // w_grp.shape[0]), F32)],
        compiler_params=pltpu.CompilerParams(
            dimension_semantics=("arbitrary", "arbitrary"),
            vmem_limit_bytes=V7X_VMEM_LIMIT_BYTES),
        name="mixer_c",
    )(h, mod, mod, mod, norm_g, hist, w_grp, b_grp, c_scale)


def _front_pad(hist, rows):
    return jnp.pad(hist, ((0, 0), (0, 0), (rows - hist.shape[2], 0), (0, 0)))


def _run_trunk(x, mod, row0, start_pos, conv_hist, pool_hist, p, ffn_weights, *, bt, tt_ffn,
               tt_mix, tf, emit_vn):
    n_layers = p["norm_g"].shape[0]
    n_rows = mod.shape[2]
    d = x.shape[-1]
    mod_tab = mod.reshape(n_layers * 9 * n_rows, 1, d)
    n_conv = conv_hist.shape[2]
    n_pool = pool_hist.shape[2]
    conv_hist = _front_pad(conv_hist, _round_up(n_conv, V7X_SUBLANES))
    pool_hist = _front_pad(pool_hist, _round_up(n_pool, V7X_SUBLANES) + V7X_SUBLANES)

    h = x
    new_conv, new_pool, new_av, bf16_weights = [], [], [], []
    for l in range(n_layers):
        def mod_row(sub, l=l):
            return lambda k: ((l * 9 + sub * 3 + k) * n_rows + row0)

        ffn = functools.partial(_ffn_call, final_g=p["final_norm_g"], bt=bt, tt=tt_ffn, tf=tf)
        h, w_first = ffn(h, mod_tab, mod_row(0), p["norm_g"][l, 0:1], ffn_weights[l][0],
                         final_norm=False)
        if l % 2 == 0:
            e = l // 2
            out = _mixer_ab_call(
                h, mod_tab, mod_row(1), p["norm_g"][l, 1:2], conv_hist[e], p["ab_w_in"][e],
                p["a_v_norm_g"][e:e + 1], p["a_v_norm_b"][e:e + 1], p["a_ws"][e], p["a_bs"][e].T,
                p["b_dw"][e], p["b_ln_g"][e:e + 1], p["b_ln_b"][e:e + 1], p["ab_w_out"][e],
                bt=bt, tt=tt_mix, emit_vn=emit_vn)
            h = out[0]
            new_conv.append(out[1][:, -n_conv:])
            if emit_vn:
                new_av.append(out[2])
        else:
            o = l // 2
            h, ph = _mixer_c_call(
                h, mod_tab, mod_row(1), p["norm_g"][l, 1:2], pool_hist[o], p["c_w_grp"][o],
                p["c_b_grp"][o].reshape(1, d), p["c_scale"][o:o + 1],
                bt=bt, tt=tt_mix, start_pos=start_pos)
            new_pool.append(ph[:, -n_pool:])
        h, w_second = ffn(h, mod_tab, mod_row(2), p["norm_g"][l, 2:3], ffn_weights[l][1],
                          final_norm=(l == n_layers - 1))
        bf16_weights.append((("bf16", w_first), ("bf16", w_second)))
    outs = (h, jnp.stack(new_conv), jnp.stack(new_pool), jnp.stack(new_av) if emit_vn else None)
    return outs, bf16_weights


def kernel(x_prompt, x_sample, c_prompt, c_sample, state_conv_b, state_pool_c, ada_w, ada_b, norm_g, final_norm_g, ffn_w_gate, ffn_w_up, ffn_w_down, ab_w_in, a_v_norm_g, a_v_norm_b, a_ws, a_bs, b_dw, b_ln_g, b_ln_b, ab_w_out, c_w_grp, c_b_grp, c_scale):
    bp, sp, d = x_prompt.shape
    bs, ss, _ = x_sample.shape
    n_layers = norm_g.shape[0]
    p = {
        "norm_g": norm_g, "final_norm_g": final_norm_g.reshape(1, d),
        "ab_w_in": ab_w_in.astype(BF16), "ab_w_out": ab_w_out.astype(BF16),
        "a_v_norm_g": a_v_norm_g, "a_v_norm_b": a_v_norm_b, "a_ws": a_ws, "a_bs": a_bs,
        "b_dw": b_dw, "b_ln_g": b_ln_g, "b_ln_b": b_ln_b,
        "c_w_grp": c_w_grp.astype(BF16), "c_b_grp": c_b_grp, "c_scale": c_scale,
    }
    n_rows = _round_up(bs + bp, V7X_SUBLANES)
    c_all = jnp.concatenate([c_sample, c_prompt, jnp.zeros((n_rows - bs - bp, d), F32)], axis=0)
    mod = _ada_call(c_all, ada_w, ada_b)

    f32_weights = [[("f32", (ffn_w_gate, ffn_w_up, ffn_w_down), l, i) for i in range(2)]
                   for l in range(n_layers)]
    (y_s, conv_s, pool_s, av_s), bf16_weights = _run_trunk(
        x_sample, mod, 0, PAST_LEN, state_conv_b, state_pool_c, p, f32_weights,
        bt=bs, tt_ffn=ss, tt_mix=ss, tf=SAMPLE_FFN_HIDDEN_TILE, emit_vn=True)

    n_even, n_odd = state_conv_b.shape[0], state_pool_c.shape[0]
    conv0 = jnp.zeros((n_even, bp) + state_conv_b.shape[2:], F32)
    pool0 = jnp.zeros((n_odd, bp) + state_pool_c.shape[2:], F32)
    (y_p, conv_p, pool_p, _), _ = _run_trunk(
        x_prompt, mod, bs, 0, conv0, pool0, p, bf16_weights,
        bt=1, tt_ffn=min(sp, PROMPT_FFN_ROWS), tt_mix=min(sp, PROMPT_MIXER_ROWS),
        tf=FFN_HIDDEN_TILE, emit_vn=False)
    return (y_p, y_s, conv_p, conv_s, pool_p, pool_s, av_s)
```

```python
import functools

import jax
import jax.numpy as jnp
from jax import lax
from jax.experimental import pallas as pl
from jax.experimental.pallas import tpu as pltpu

EPS = 1e-6
STREAM_CHUNK = 64
POOL_WINDOWS = (2, 4, 8, 16)
PAST_LEN = 1024

V7X_SUBLANES = 8
V7X_LANES = 128
V7X_MXU_DIM = 256
V7X_VMEM_LIMIT_BYTES = 56 * 1024 * 1024

PROMPT_FFN_ROWS = 1024
PROMPT_MIXER_ROWS = 256
PROMPT_POOL_ROWS = 512
SAMPLE_MIXER_STREAMS = 4
FFN_HIDDEN_TILE = 512
SAMPLE_FFN_HIDDEN_TILE = 256
FFN_NORM_SPREAD_STEPS = 8
MAX_UNROLLED_CONV_BLOCKS = 32

F32 = jnp.float32
BF16 = jnp.bfloat16


def _round_up(n, m):
    return (n + m - 1) // m * m


def _rms_mod(h, g, shift, scale):
    y = h * lax.rsqrt(jnp.mean(h * h, axis=-1, keepdims=True) + EPS) * g
    return y * (1 + scale) + shift


def _layer_norm(x, g, b):
    mu = jnp.mean(x, axis=-1, keepdims=True)
    d = x - mu
    var = jnp.mean(d * d, axis=-1, keepdims=True)
    return d * lax.rsqrt(var + EPS) * g + b


def _dot(a, b):
    return jnp.dot(a, b, preferred_element_type=F32)


def _sigmoid(x):
    return 0.5 * jnp.tanh(0.5 * x) + 0.5


def _silu(x):
    return x * _sigmoid(x)


def _ada_kernel(c_ref, w_ref, b_ref, o_ref):
    sc = _silu(c_ref[...]).astype(BF16)
    o_ref[...] = _dot(sc, w_ref[...].astype(BF16)) + b_ref[...]


def _ada_call(c_all, ada_w, ada_b):
    n_layers, d, nd = ada_w.shape
    n_comp = nd // d
    rows = c_all.shape[0]
    tn = d // 2
    per = d // tn
    return pl.pallas_call(
        _ada_kernel,
        grid=(n_layers, nd // tn),
        in_specs=[
            pl.BlockSpec((rows, d), lambda l, n: (0, 0)),
            pl.BlockSpec((None, d, tn), lambda l, n: (l, 0, n)),
            pl.BlockSpec((None, 1, tn), lambda l, n: (l, 0, n)),
        ],
        out_specs=pl.BlockSpec((None, None, rows, tn), lambda l, n: (l, n // per, 0, n % per)),
        out_shape=jax.ShapeDtypeStruct((n_layers, n_comp, rows, d), F32),
        compiler_params=pltpu.CompilerParams(
            dimension_semantics=("arbitrary", "arbitrary"),
            vmem_limit_bytes=V7X_VMEM_LIMIT_BYTES),
        name="ada_proj",
    )(c_all, ada_w, ada_b.reshape(n_layers, 1, nd))


def _ffn_kernel(h_hbm, shift_ref, scale_ref, gate_ref, nshift_ref, nscale_ref, ng_ref, wg_ref, wu_ref,
                wd_ref, fg_ref, o_ref, *rest, final_norm, norm_rows, emit_bf16_weights, spread_steps):
    if emit_bf16_weights:
        wg_out, wu_out, wd_out, *rest = rest
    hbuf_ref, *hn_refs, sem = rest
    b, t, j = pl.program_id(0), pl.program_id(1), pl.program_id(2)
    nb, nt, nj = pl.num_programs(0), pl.num_programs(1), pl.num_programs(2)
    bt, tt, d = o_ref.shape
    spread = spread_steps > 0
    tile = b * nt + t
    is_first, is_last = tile == 0, tile == nb * nt - 1
    not_last = jnp.logical_not(is_last)
    last_t = t == nt - 1
    next_b, next_t = jnp.where(last_t, b + 1, b), jnp.where(last_t, 0, t + 1)

    def h_copy(bb, ti):
        return pltpu.make_async_copy(
            h_hbm.at[pl.ds(bb * bt, bt), pl.ds(ti * tt, tt), :], hbuf_ref, sem)

    def norm_chunk(dst_ref, s, r, gmul, shift, init_acc):
        x = hbuf_ref[s, pl.ds(r, norm_rows), :]
        inv = lax.rsqrt(jnp.mean(x * x, axis=-1, keepdims=True) + EPS)
        y = ((x * inv) * gmul + shift).astype(BF16)
        dst_ref[pl.ds(s * tt + r, norm_rows), :] = y
        if init_acc:
            o_ref[s, pl.ds(r, norm_rows), :] = x
        return y

    def whole_tile_prologue():
        for s in range(bt):
            gmul = ng_ref[...] * (1 + scale_ref[s])
            shift = shift_ref[s]

            def rows(i, carry, s=s, gmul=gmul, shift=shift):
                norm_chunk(hn_refs[0], s, pl.multiple_of(i * norm_rows, norm_rows), gmul, shift, True)
                return carry

            lax.fori_loop(0, tt // norm_rows, rows, 0, unroll=4)

    @pl.when(j == 0)
    def _():
        @pl.when(is_first)
        def _():
            h_copy(b, t).start()

        if spread:
            @pl.when(is_first)
            def _():
                h_copy(b, t).wait()
                whole_tile_prologue()

            @pl.when(jnp.logical_not(is_first))
            def _():
                o_ref[...] = hbuf_ref[...]
        else:
            h_copy(b, t).wait()
            whole_tile_prologue()

    @pl.when((j == jnp.minimum(1, nj - 1)) & not_last)
    def _():
        h_copy(next_b, next_t).start()

    if spread:
        @pl.when((j == 2) & not_last)
        def _():
            h_copy(next_b, next_t).wait()

    def step(norm_next, parity):
        if emit_bf16_weights:
            wg, wu, wd = (w[...].astype(BF16) for w in (wg_ref, wu_ref, wd_ref))
            wg_out[...], wu_out[...], wd_out[...] = wg, wu, wd
        else:
            wg, wu, wd = wg_ref[...], wu_ref[...], wd_ref[...]
        hn = hn_refs[parity][...]
        gated = _dot(hn, wg)
        if norm_next:
            rows_per_step = tt // spread_steps
            n_sub = rows_per_step // norm_rows
            blk = bt * tt // n_sub
            gmul = ng_ref[...] * (1 + nscale_ref[0])
            shift = nshift_ref[0]
            pieces = []
            for k in range(n_sub):
                r = pl.multiple_of((j - 2) * rows_per_step + k * norm_rows, norm_rows)
                y = norm_chunk(hn_refs[1 - parity], 0, r, gmul, shift, False)
                bits = lax.bitcast_convert_type(y[:, 0:gated.shape[1]].astype(F32), jnp.int32)
                zero = lax.shift_right_logical(lax.shift_right_logical(bits, 16), 16).astype(F32)
                pieces += [gated[k * blk:k * blk + norm_rows] + zero,
                           gated[k * blk + norm_rows:(k + 1) * blk]]
            gated = jnp.concatenate(pieces, axis=0)
        act = (_silu(gated) * _dot(hn, wu)).astype(BF16)
        o_ref[...] += (0.5 * gate_ref[...]) * _dot(act, wd).reshape(bt, tt, d)

    if spread:
        in_window = (j >= 2) & (j < 2 + spread_steps) & not_last
        for parity in range(2):
            mine = tile % 2 == parity
            pl.when(mine & in_window)(functools.partial(step, True, parity))
            pl.when(mine & jnp.logical_not(in_window))(functools.partial(step, False, parity))
    else:
        step(False, 0)

    if final_norm:
        @pl.when(j == nj - 1)
        def _():
            y = o_ref[...]
            o_ref[...] = (y * lax.rsqrt(jnp.mean(y * y, axis=-1, keepdims=True) + EPS)) * fg_ref[...]


def _ffn_call(h, mod, mod_row, norm_g, weights, final_g, *, bt, tt, tf, final_norm):
    bsz, seq, d = h.shape
    kind, (wg, wu, wd) = weights[0], weights[1]
    ff = wg.shape[-1]
    grid = (bsz // bt, seq // tt, ff // tf)
    mspec = [pl.BlockSpec((bt, 1, d), functools.partial(_mod_index, row=mod_row(k), bt=bt))
             for k in range(3)]
    norm_rows = 16
    assert seq % tt == 0 and bsz % bt == 0 and ff % tf == 0 and tt % (4 * norm_rows) == 0
    spread_steps = FFN_NORM_SPREAD_STEPS
    if not (bt == 1 and grid[0] * grid[1] > 1 and grid[2] >= spread_steps + 2
            and tt % (spread_steps * norm_rows) == 0):
        spread_steps = 0
    next_index = lambda b, t, j: (jnp.where(t == grid[1] - 1, jnp.minimum(b + 1, grid[0] - 1), b),)
    nspec = [pl.BlockSpec((bt, 1, d), lambda b, t, j, k=k: _mod_index(
        *next_index(b, t, j), row=mod_row(k), bt=bt)) for k in range(2)]
    up_spec = pl.BlockSpec((d, tf), lambda b, t, j: (0, j))
    down_spec = pl.BlockSpec((tf, d), lambda b, t, j: (j, 0))
    out_specs = [pl.BlockSpec((bt, tt, d), lambda b, t, j: (b, t, 0))]
    out_shape = [jax.ShapeDtypeStruct(h.shape, F32)]
    if kind == "f32":
        l, i = weights[2], weights[3]
        assert grid[:2] == (1, 1)
        w_specs = [pl.BlockSpec((None, None, d, tf), lambda b, t, j: (l, i, 0, j)),
                   pl.BlockSpec((None, None, d, tf), lambda b, t, j: (l, i, 0, j)),
                   pl.BlockSpec((None, None, tf, d), lambda b, t, j: (l, i, j, 0))]
        out_specs += [up_spec, up_spec, down_spec]
        out_shape += [jax.ShapeDtypeStruct((d, ff), BF16), jax.ShapeDtypeStruct((d, ff), BF16),
                      jax.ShapeDtypeStruct((ff, d), BF16)]
    else:
        w_specs = [up_spec, up_spec, down_spec]
    out = pl.pallas_call(
        functools.partial(_ffn_kernel, final_norm=final_norm, norm_rows=norm_rows,
                          emit_bf16_weights=(kind == "f32"), spread_steps=spread_steps),
        grid=grid,
        in_specs=[
            pl.BlockSpec(memory_space=pl.ANY),
            *mspec,
            *nspec,
            pl.BlockSpec((1, d), lambda b, t, j: (0, 0)),
            *w_specs,
            pl.BlockSpec((1, d), lambda b, t, j: (0, 0)),
        ],
        out_specs=out_specs,
        out_shape=out_shape,
        scratch_shapes=[pltpu.VMEM((bt, tt, d), F32),
                        *[pltpu.VMEM((bt * tt, d), BF16)] * (2 if spread_steps else 1),
                        pltpu.SemaphoreType.DMA(())],
        compiler_params=pltpu.CompilerParams(
            dimension_semantics=("arbitrary", "arbitrary", "arbitrary"),
            vmem_limit_bytes=V7X_VMEM_LIMIT_BYTES),
        name="ffn_half_step",
    )(h, mod, mod, mod, mod, mod, norm_g, wg, wu, wd, final_g)
    return out[0], (tuple(out[1:]) if kind == "f32" else (wg, wu, wd))


def _mod_index(b, *_, row, bt):
    return (row // bt + b, 0, 0)


def _mixer_ab_front(t, last_t, h_ref, shift_ref, scale_ref, ng_ref, hist_ref, win_ref, vg_ref, vb_ref,
                    ws_ref, bst_ref, dw_ref, lng_ref, lnb_ref, convnew_ref, vn_ref, xpad_ref, conv_ref,
                    xsh_ref, dwb_ref, aout_ref, bout_ref, hres_ref, *, conv_rows, conv_cols,
                    unroll_conv, midway=None):
    bt, tt, d = h_ref.shape
    aw = vg_ref.shape[-1]
    bw = lng_ref.shape[-1]
    n_grp, a_chunk, _ = ws_ref.shape
    gd = aw // n_grp
    n_taps = dw_ref.shape[0]
    hpad = xpad_ref.shape[1] - tt
    off = hpad - (n_taps - 1)
    cr = min(a_chunk, tt)

    @pl.when(t == 0)
    def _():
        xpad_ref[:, 0:hpad, :] = hist_ref[...]
        for k in range(n_taps):
            dwb_ref[k] = jnp.broadcast_to(dw_ref[k:k + 1, :], (V7X_SUBLANES, bw))

    h = h_ref[...]
    hn = _rms_mod(h, ng_ref[...], shift_ref[...], scale_ref[...]).astype(BF16).reshape(bt * tt, d)

    ba = _dot(hn, win_ref[:, 2 * aw:2 * aw + bw])
    bg = _dot(hn, win_ref[:, 2 * aw + bw:2 * aw + 2 * bw])
    xpad_ref[:, hpad:hpad + tt, :] = (ba * _sigmoid(bg)).reshape(bt, tt, bw)

    n_sh = xsh_ref.shape[1]

    def conv_block(s, r0, cols):
        tiles = conv_rows // V7X_SUBLANES
        acc = jnp.zeros((tiles, V7X_SUBLANES, conv_cols), F32)
        for k in range(n_taps):
            r = (off + k) % V7X_SUBLANES
            a = (off + k) - r
            if r == 0:
                x = xpad_ref[s, pl.ds(r0 + a, conv_rows), cols]
            else:
                x = xsh_ref[r - 1, pl.ds(r0 + a, conv_rows), cols]
            acc = acc + x.reshape(tiles, V7X_SUBLANES, conv_cols) * dwb_ref[k, :, cols]
        conv_ref[pl.ds(s * tt + r0, conv_rows), cols] = acc.reshape(conv_rows, conv_cols)

    for s in range(bt):
        for r in range(1, V7X_SUBLANES):
            xsh_ref[r - 1] = xpad_ref[s, r:r + n_sh, :]
        for c0 in range(0, bw, conv_cols):
            cols = slice(c0, c0 + conv_cols)
            if unroll_conv:
                for r0 in range(0, tt, conv_rows):
                    conv_block(s, r0, cols)
            else:
                def conv_body(rb, carry, s=s, cols=cols):
                    conv_block(s, pl.multiple_of(rb * conv_rows, conv_rows), cols)
                    return carry

                lax.fori_loop(0, tt // conv_rows, conv_body, 0)

    u = jax.nn.gelu(_dot(hn, win_ref[:, 0:aw]))
    v = jax.nn.gelu(_dot(hn, win_ref[:, aw:2 * aw]))
    tie = midway(v) if midway is not None else None
    vn = _layer_norm(v, vg_ref[...], vb_ref[...])
    if vn_ref is not None:
        vn_ref[...] = vn.reshape(bt, tt, aw)
    vnb = vn.astype(BF16)
    gate_rows = min(V7X_MXU_DIM, bt * tt)
    row = lax.broadcasted_iota(jnp.int32, (a_chunk, a_chunk), 0)
    col = lax.broadcasted_iota(jnp.int32, (a_chunk, a_chunk), 1)
    keep = (row < cr) & (col < cr) & (col // STREAM_CHUNK <= row // STREAM_CHUNK)
    for g in range(n_grp):
        corner = jnp.where(keep, ws_ref[g], 0.0)
        diag = corner
        for i in range(1, a_chunk // cr):
            diag = diag + pltpu.roll(pltpu.roll(corner, i * cr, 0), i * cr, 1)
        diag = diag.astype(BF16)
        nb = gate_rows // a_chunk
        zero = jnp.zeros_like(diag)
        wbd = diag if nb <= 1 else jnp.concatenate(
            [jnp.concatenate([diag if i == j else zero for j in range(nb)], axis=1)
             for i in range(nb)], axis=0)
        wbd = wbd[0:gate_rows, 0:gate_rows]
        bias = jnp.concatenate([bst_ref[0:cr, g:g + 1]] * (gate_rows // cr), axis=0)
        for r0 in range(0, bt * tt, gate_rows):
            gate = _dot(wbd, vnb[r0:r0 + gate_rows, g * gd:(g + 1) * gd]) + bias
            aout_ref[r0:r0 + gate_rows, g * gd:(g + 1) * gd] = (
                u[r0:r0 + gate_rows, g * gd:(g + 1) * gd] * gate).astype(BF16)

    conv = conv_ref[...]
    if tie is not None:
        tie_row = jnp.concatenate([tie] * (bw // V7X_LANES), axis=1)
        conv = (conv.reshape(bt * tt // V7X_SUBLANES, V7X_SUBLANES, bw) + tie_row).reshape(bt * tt, bw)
    bout_ref[...] = _silu(_layer_norm(conv, lng_ref[...], lnb_ref[...])).astype(BF16)
    if hres_ref is not None:
        hres_ref[...] = h

    tail = xpad_ref[:, tt:tt + hpad, :]
    xpad_ref[:, 0:hpad, :] = tail

    @pl.when(last_t)
    def _():
        convnew_ref[...] = tail


def _exact_zero_tile(x):
    bits = lax.bitcast_convert_type(x[0:V7X_SUBLANES, 0:V7X_LANES].astype(F32), jnp.int32)
    return lax.shift_right_logical(lax.shift_right_logical(bits, 16), 16).astype(F32)


def _mixer_ab_back(gate_ref, wout_ref, o_ref, aout_ref, bout_ref, hres_ref, tie_rows=32, after=None):
    bt, tt, d = o_ref.shape
    aw, bw = aout_ref.shape[-1], bout_ref.shape[-1]
    aout = aout_ref[...]
    if after is not None:
        n_r, n_c = after.shape[0] - V7X_SUBLANES, after.shape[1] - V7X_LANES
        z = sum(_exact_zero_tile(after[r:, c:]) for r in (0, n_r) for c in (0, n_c))
        z = jnp.concatenate([jnp.concatenate([z, z], axis=0)] * (aw // V7X_LANES), axis=1).astype(BF16)
        rows = z.shape[0]
        aout = (aout.reshape(bt * tt // rows, rows, aw) + z).reshape(bt * tt, aw)
    y = _dot(aout, wout_ref[0:aw, :]) + _dot(bout_ref[...], wout_ref[aw:aw + bw, :])
    o_ref[...] = hres_ref[...] + gate_ref[...] * y.reshape(bt, tt, d)
    zero = jnp.zeros((V7X_SUBLANES, V7X_LANES), jnp.int32)
    for r0 in range(0, bt * tt, tie_rows):
        for c0 in range(0, d, V7X_MXU_DIM):
            bits = lax.bitcast_convert_type(y[r0:r0 + V7X_SUBLANES, c0:c0 + V7X_LANES], jnp.int32)
            zero = zero + lax.shift_right_logical(lax.shift_right_logical(bits, 16), 16)
    return zero.astype(F32)


def _mixer_ab_kernel(h_ref, shift_ref, scale_ref, gate_ref, ng_ref, hist_ref, win_ref, vg_ref,
                     vb_ref, ws_ref, bst_ref, dw_ref, lng_ref, lnb_ref, wout_ref,
                     o_ref, convnew_ref, *rest, emit_vn, n_tiles, tiles_per_stream, **conv_params):
    vn_ref = rest[0] if emit_vn else None
    xpad_ref, conv_ref, xsh_ref, dwb_ref, *handover = rest[1:] if emit_vn else rest
    if n_tiles > 1:
        handover = (handover[0:3], handover[3:6])
    else:
        handover = ((*handover, None),)
    q = pl.program_id(0)
    t = q % tiles_per_stream

    def front(parity, midway=None):
        _mixer_ab_front(t, t == tiles_per_stream - 1, h_ref, shift_ref, scale_ref, ng_ref, hist_ref,
                        win_ref, vg_ref, vb_ref, ws_ref, bst_ref, dw_ref, lng_ref, lnb_ref, convnew_ref,
                        vn_ref, xpad_ref, conv_ref, xsh_ref, dwb_ref, *handover[parity], midway=midway,
                        **conv_params)

    def back(parity, after=None):
        aout_ref, bout_ref, hres_ref = handover[parity]
        return _mixer_ab_back(gate_ref, wout_ref, o_ref, aout_ref, bout_ref,
                              h_ref if hres_ref is None else hres_ref, after=after)

    @pl.when(q == 0)
    def _():
        front(0)

    if n_tiles > 1:
        for parity in range(2):
            @pl.when((q > 0) & (q < n_tiles) & (q % 2 == parity))
            def _(parity=parity):
                front(parity, midway=functools.partial(back, 1 - parity))

    @pl.when(q == n_tiles)
    def _():
        back((n_tiles - 1) % 2)


def _mixer_ab_call(h, mod, mod_row, norm_g, hist, w_in, v_g, v_b, ws, bs_t, dw, ln_g, ln_b, w_out,
                   *, bt, tt, emit_vn):
    bsz, seq, d = h.shape
    aw, bw = v_g.shape[-1], ln_g.shape[-1]
    hpad = hist.shape[1]
    conv_rows = min(tt, 32)
    conv_cols = 256
    unroll_conv = bt == 1 and (bw // conv_cols) * (tt // conv_rows) <= MAX_UNROLLED_CONV_BLOCKS
    once = dict(pipeline_mode=pl.Buffered(1))
    full2 = lambda a: pl.BlockSpec(a.shape, lambda q: (0, 0))
    nt = seq // tt
    n_tiles = (bsz // bt) * nt
    front_tile = lambda q: jnp.minimum(q, n_tiles - 1)
    back_tile = lambda q: jnp.maximum(q - 1, 0)
    stream_of = lambda tile: tile // nt
    front_mod = [pl.BlockSpec((bt, 1, d), lambda q, k=k: _mod_index(
        stream_of(front_tile(q)), row=mod_row(k), bt=bt)) for k in range(2)]
    back_mod = pl.BlockSpec((bt, 1, d), lambda q: _mod_index(
        stream_of(back_tile(q)), row=mod_row(2), bt=bt))
    front_rows = lambda q: (stream_of(front_tile(q)), front_tile(q) % nt, 0)
    out_specs = [pl.BlockSpec((bt, tt, d), lambda q: (stream_of(back_tile(q)), back_tile(q) % nt, 0)),
                 pl.BlockSpec((bt, hpad, bw), lambda q: (stream_of(front_tile(q)), 0, 0))]
    out_shape = [jax.ShapeDtypeStruct(h.shape, F32),
                 jax.ShapeDtypeStruct((bsz, hpad, bw), F32)]
    if emit_vn:
        out_specs.append(pl.BlockSpec((bt, tt, aw), front_rows))
        out_shape.append(jax.ShapeDtypeStruct((bsz, seq, aw), F32))
    single = once if n_tiles == 1 else {}
    handover = [pltpu.VMEM((bt * tt, aw), BF16), pltpu.VMEM((bt * tt, bw), BF16)]
    if n_tiles > 1:
        handover = 2 * (handover + [pltpu.VMEM((bt, tt, d), F32)])
    return pl.pallas_call(
        functools.partial(_mixer_ab_kernel, emit_vn=emit_vn, n_tiles=n_tiles, tiles_per_stream=nt,
                          conv_rows=conv_rows, conv_cols=conv_cols, unroll_conv=unroll_conv),
        grid=(n_tiles + 1,),
        in_specs=[
            pl.BlockSpec((bt, tt, d), front_rows, **single),
            *front_mod, back_mod,
            full2(norm_g),
            pl.BlockSpec((bt, hpad, bw), lambda q: (stream_of(front_tile(q)), 0, 0), **single),
            pl.BlockSpec(w_in.shape, lambda q: (0, 0), **once),
            full2(v_g), full2(v_b),
            pl.BlockSpec(ws.shape, lambda q: (0, 0, 0)),
            full2(bs_t), full2(dw), full2(ln_g), full2(ln_b),
            pl.BlockSpec(w_out.shape, lambda q: (0, 0), **once),
        ],
        out_specs=out_specs,
        out_shape=out_shape,
        scratch_shapes=[pltpu.VMEM((bt, hpad + tt, bw), F32),
                        pltpu.VMEM((bt * tt, bw), F32),
                        pltpu.VMEM((V7X_SUBLANES - 1, hpad + tt - V7X_SUBLANES, bw), F32),
                        pltpu.VMEM((dw.shape[0], V7X_SUBLANES, bw), F32),
                        *handover],
        compiler_params=pltpu.CompilerParams(
            dimension_semantics=("arbitrary",),
            vmem_limit_bytes=V7X_VMEM_LIMIT_BYTES),
        name="mixer_ab",
    )(h, mod, mod, mod, norm_g, hist, w_in, v_g, v_b, ws, bs_t, dw, ln_g, ln_b, w_out)


def _mixer_c_kernel(h_ref, shift_ref, scale_ref, gate_ref, ng_ref, hist_ref, wgrp_ref, bgrp_ref,
                    cs_ref, o_ref, poolnew_ref, xpad_ref, y_ref, lvl_ref, *, start_pos):
    t = pl.program_id(1)
    bt, tt, d = h_ref.shape
    n_grp = wgrp_ref.shape[0]
    gd = d // n_grp
    hpad = xpad_ref.shape[1] - tt

    @pl.when(t == 0)
    def _():
        xpad_ref[:, 0:hpad, :] = hist_ref[...]

    h = h_ref[...]
    hn = _rms_mod(h, ng_ref[...], shift_ref[...], scale_ref[...])
    xpad_ref[:, hpad:hpad + tt, :] = hn

    end = hpad + tt

    def window_sum(cols, w):
        m = w.bit_length() - 1
        assert w == 1 << m and hpad - w + 1 >= 0

        def read(level, row0, n):
            if level == 0:
                return xpad_ref[:, row0:row0 + n, cols]
            return lvl_ref[(level - 1) % 2, :, row0:row0 + n, :]

        for i in range(1, m):
            k = 1 << (i - 1)
            lo = (hpad - w + (1 << i)) // V7X_SUBLANES * V7X_SUBLANES
            buf = (i - 1) % 2
            if lo > 0:
                lvl_ref[buf, :, 0:lo, :] = jnp.zeros((bt, lo, gd), F32)
            lvl_ref[buf, :, lo:end, :] = read(i - 1, lo, end - lo) + read(i - 1, lo - k, end - lo)
        k = 1 << (m - 1)
        return read(m - 1, hpad, tt) + read(m - 1, hpad - k, tt)

    pos = start_pos + t * tt + lax.broadcasted_iota(jnp.int32, (1, tt, 1), 1)
    for g, w in enumerate(POOL_WINDOWS):
        cols = slice(g * gd, (g + 1) * gd)
        s = window_sum(cols, w)
        cnt = jnp.minimum(w, pos + 1).astype(F32)
        pooled = (s / cnt - hn[:, :, cols]).astype(BF16).reshape(bt * tt, gd)
        y_ref[:, cols] = _dot(pooled, wgrp_ref[g])
    y = (y_ref[...] + bgrp_ref[...]) * cs_ref[...]
    o_ref[...] = h + gate_ref[...] * y.reshape(bt, tt, d)

    tail = xpad_ref[:, tt:tt + hpad, :]
    xpad_ref[:, 0:hpad, :] = tail

    @pl.when(t == pl.num_programs(1) - 1)
    def _():
        poolnew_ref[...] = tail


def _mixer_c_call(h, mod, mod_row, norm_g, hist, w_grp, b_grp, c_scale, *, bt, tt, start_pos):
    bsz, seq, d = h.shape
    hpad = hist.shape[1]
    full2 = lambda a: pl.BlockSpec(a.shape, lambda b, t: (0, 0))
    mspec = [pl.BlockSpec((bt, 1, d), functools.partial(_mod_index, row=mod_row(k), bt=bt))
             for k in range(3)]
    return pl.pallas_call(
        functools.partial(_mixer_c_kernel, start_pos=start_pos),
        grid=(bsz // bt, seq // tt),
        in_specs=[
            pl.BlockSpec((bt, tt, d), lambda b, t: (b, t, 0)),
            *mspec,
            full2(norm_g),
            pl.BlockSpec((bt, hpad, d), lambda b, t: (b, 0, 0)),
            pl.BlockSpec(w_grp.shape, lambda b, t: (0, 0, 0)),
            full2(b_grp), full2(c_scale),
        ],
        out_specs=[pl.BlockSpec((bt, tt, d), lambda b, t: (b, t, 0)),
                   pl.BlockSpec((bt, hpad, d), lambda b, t: (b, 0, 0))],
        out_shape=[jax.ShapeDtypeStruct(h.shape, F32),
                   jax.ShapeDtypeStruct((bsz, hpad, d), F32)],
        scratch_shapes=[pltpu.VMEM((bt, hpad + tt, d), F32),
                        pltpu.VMEM((bt * tt, d), F32),
                        pltpu.VMEM((2, bt, hpad + tt, d // w_grp.shape[0]), F32)],
        compiler_params=pltpu.CompilerParams(
            dimension_semantics=("arbitrary", "arbitrary"),
            vmem_limit_bytes=V7X_VMEM_LIMIT_BYTES),
        name="mixer_c",
    )(h, mod, mod, mod, norm_g, hist, w_grp, b_grp, c_scale)


def _front_pad(hist, rows):
    return jnp.pad(hist, ((0, 0), (0, 0), (rows - hist.shape[2], 0), (0, 0)))


def _run_trunk(x, mod, row0, start_pos, conv_hist, pool_hist, p, ffn_weights, *, bt, tt_ffn,
               bt_mix, tt_mix, tt_pool, tf, emit_vn):
    n_layers = p["norm_g"].shape[0]
    n_rows = mod.shape[2]
    d = x.shape[-1]
    mod_tab = mod.reshape(n_layers * 9 * n_rows, 1, d)
    n_conv = conv_hist.shape[2]
    n_pool = pool_hist.shape[2]
    conv_hist = _front_pad(conv_hist, _round_up(n_conv, V7X_SUBLANES))
    pool_hist = _front_pad(pool_hist, _round_up(n_pool, V7X_SUBLANES) + V7X_SUBLANES)

    h = x
    new_conv, new_pool, new_av, bf16_weights = [], [], [], []
    for l in range(n_layers):
        def mod_row(sub, l=l):
            return lambda k: ((l * 9 + sub * 3 + k) * n_rows + row0)

        ffn = functools.partial(_ffn_call, final_g=p["final_norm_g"], bt=bt, tt=tt_ffn, tf=tf)
        h, w_first = ffn(h, mod_tab, mod_row(0), p["norm_g"][l, 0:1], ffn_weights[l][0],
                         final_norm=False)
        if l % 2 == 0:
            e = l // 2
            out = _mixer_ab_call(
                h, mod_tab, mod_row(1), p["norm_g"][l, 1:2], conv_hist[e], p["ab_w_in"][e],
                p["a_v_norm_g"][e:e + 1], p["a_v_norm_b"][e:e + 1], p["a_ws"][e], p["a_bs"][e].T,
                p["b_dw"][e], p["b_ln_g"][e:e + 1], p["b_ln_b"][e:e + 1], p["ab_w_out"][e],
                bt=bt_mix, tt=tt_mix, emit_vn=emit_vn)
            h = out[0]
            new_conv.append(out[1][:, -n_conv:])
            if emit_vn:
                new_av.append(out[2])
        else:
            o = l // 2
            h, ph = _mixer_c_call(
                h, mod_tab, mod_row(1), p["norm_g"][l, 1:2], pool_hist[o], p["c_w_grp"][o],
                p["c_b_grp"][o].reshape(1, d), p["c_scale"][o:o + 1],
                bt=bt, tt=tt_pool, start_pos=start_pos)
            new_pool.append(ph[:, -n_pool:])
        h, w_second = ffn(h, mod_tab, mod_row(2), p["norm_g"][l, 2:3], ffn_weights[l][1],
                          final_norm=(l == n_layers - 1))
        bf16_weights.append((("bf16", w_first), ("bf16", w_second)))
    outs = (h, jnp.stack(new_conv), jnp.stack(new_pool), jnp.stack(new_av) if emit_vn else None)
    return outs, bf16_weights


def kernel(x_prompt, x_sample, c_prompt, c_sample, state_conv_b, state_pool_c, ada_w, ada_b, norm_g, final_norm_g, ffn_w_gate, ffn_w_up, ffn_w_down, ab_w_in, a_v_norm_g, a_v_norm_b, a_ws, a_bs, b_dw, b_ln_g, b_ln_b, ab_w_out, c_w_grp, c_b_grp, c_scale):
    bp, sp, d = x_prompt.shape
    bs, ss, _ = x_sample.shape
    n_layers = norm_g.shape[0]
    p = {
        "norm_g": norm_g, "final_norm_g": final_norm_g.reshape(1, d),
        "ab_w_in": ab_w_in.astype(BF16), "ab_w_out": ab_w_out.astype(BF16),
        "a_v_norm_g": a_v_norm_g, "a_v_norm_b": a_v_norm_b, "a_ws": a_ws, "a_bs": a_bs,
        "b_dw": b_dw, "b_ln_g": b_ln_g, "b_ln_b": b_ln_b,
        "c_w_grp": c_w_grp.astype(BF16), "c_b_grp": c_b_grp, "c_scale": c_scale,
    }
    n_rows = _round_up(bs + bp, V7X_SUBLANES)
    c_all = jnp.concatenate([c_sample, c_prompt, jnp.zeros((n_rows - bs - bp, d), F32)], axis=0)
    mod = _ada_call(c_all, ada_w, ada_b)

    f32_weights = [[("f32", (ffn_w_gate, ffn_w_up, ffn_w_down), l, i) for i in range(2)]
                   for l in range(n_layers)]
    (y_s, conv_s, pool_s, av_s), bf16_weights = _run_trunk(
        x_sample, mod, 0, PAST_LEN, state_conv_b, state_pool_c, p, f32_weights,
        bt=bs, tt_ffn=ss, bt_mix=min(bs, SAMPLE_MIXER_STREAMS), tt_mix=ss, tt_pool=ss,
        tf=SAMPLE_FFN_HIDDEN_TILE, emit_vn=True)

    n_even, n_odd = state_conv_b.shape[0], state_pool_c.shape[0]
    conv0 = jnp.zeros((n_even, bp) + state_conv_b.shape[2:], F32)
    pool0 = jnp.zeros((n_odd, bp) + state_pool_c.shape[2:], F32)
    (y_p, conv_p, pool_p, _), _ = _run_trunk(
        x_prompt, mod, bs, 0, conv0, pool0, p, bf16_weights,
        bt=1, tt_ffn=min(sp, PROMPT_FFN_ROWS), bt_mix=1, tt_mix=min(sp, PROMPT_MIXER_ROWS),
        tt_pool=min(sp, PROMPT_POOL_ROWS), tf=FFN_HIDDEN_TILE, emit_vn=False)
    return (y_p, y_s, conv_p, conv_s, pool_p, pool_s, av_s)
```

```python
import functools

import jax
import jax.numpy as jnp
from jax import lax
from jax.experimental import pallas as pl
from jax.experimental.pallas import tpu as pltpu

EPS = 1e-6
STREAM_CHUNK = 64
POOL_WINDOWS = (2, 4, 8, 16)
PAST_LEN = 1024

V7X_SUBLANES = 8
V7X_LANES = 128
V7X_MXU_DIM = 256
V7X_VMEM_LIMIT_BYTES = 56 * 1024 * 1024

PROMPT_FFN_ROWS = 1024
PROMPT_MIXER_ROWS = 256
PROMPT_POOL_ROWS = 512
SAMPLE_MIXER_STREAMS = 4
FFN_HIDDEN_TILE = 512
SAMPLE_FFN_HIDDEN_TILE = 256
FFN_NORM_SPREAD_STEPS = 8
MAX_UNROLLED_CONV_BLOCKS = 32

F32 = jnp.float32
BF16 = jnp.bfloat16


def _round_up(n, m):
    return (n + m - 1) // m * m


def _rms_mod(h, g, shift, scale):
    y = h * lax.rsqrt(jnp.mean(h * h, axis=-1, keepdims=True) + EPS) * g
    return y * (1 + scale) + shift


def _layer_norm(x, g, b):
    mu = jnp.mean(x, axis=-1, keepdims=True)
    d = x - mu
    var = jnp.mean(d * d, axis=-1, keepdims=True)
    return d * lax.rsqrt(var + EPS) * g + b


def _dot(a, b):
    return jnp.dot(a, b, preferred_element_type=F32)


def _sigmoid(x):
    return 0.5 * jnp.tanh(0.5 * x) + 0.5


def _silu(x):
    return x * _sigmoid(x)


def _ada_kernel(c_ref, w_ref, b_ref, o_ref):
    sc = _silu(c_ref[...]).astype(BF16)
    o_ref[...] = _dot(sc, w_ref[...].astype(BF16)) + b_ref[...]


def _ada_call(c_all, ada_w, ada_b):
    n_layers, d, nd = ada_w.shape
    n_comp = nd // d
    rows = c_all.shape[0]
    tn = d // 2
    per = d // tn
    return pl.pallas_call(
        _ada_kernel,
        grid=(n_layers, nd // tn),
        in_specs=[
            pl.BlockSpec((rows, d), lambda l, n: (0, 0)),
            pl.BlockSpec((None, d, tn), lambda l, n: (l, 0, n)),
            pl.BlockSpec((None, 1, tn), lambda l, n: (l, 0, n)),
        ],
        out_specs=pl.BlockSpec((None, None, rows, tn), lambda l, n: (l, n // per, 0, n % per)),
        out_shape=jax.ShapeDtypeStruct((n_layers, n_comp, rows, d), F32),
        compiler_params=pltpu.CompilerParams(
            dimension_semantics=("arbitrary", "arbitrary"),
            vmem_limit_bytes=V7X_VMEM_LIMIT_BYTES),
        name="ada_proj",
    )(c_all, ada_w, ada_b.reshape(n_layers, 1, nd))


def _ffn_kernel(h_hbm, shift_ref, scale_ref, gate_ref, nshift_ref, nscale_ref, ng_ref, wg_ref, wu_ref,
                wd_ref, fg_ref, o_ref, *rest, final_norm, norm_rows, emit_bf16_weights, spread_steps):
    if emit_bf16_weights:
        wg_out, wu_out, wd_out, *rest = rest
    hbuf_ref, *hn_refs, sem = rest
    b, t, j = pl.program_id(0), pl.program_id(1), pl.program_id(2)
    nb, nt, nj = pl.num_programs(0), pl.num_programs(1), pl.num_programs(2)
    bt, tt, d = o_ref.shape
    spread = spread_steps > 0
    tile = b * nt + t
    is_first, is_last = tile == 0, tile == nb * nt - 1
    not_last = jnp.logical_not(is_last)
    last_t = t == nt - 1
    next_b, next_t = jnp.where(last_t, b + 1, b), jnp.where(last_t, 0, t + 1)

    def h_copy(bb, ti):
        return pltpu.make_async_copy(
            h_hbm.at[pl.ds(bb * bt, bt), pl.ds(ti * tt, tt), :], hbuf_ref, sem)

    def norm_chunk(dst_ref, s, r, gmul, shift, init_acc):
        x = hbuf_ref[s, pl.ds(r, norm_rows), :]
        inv = lax.rsqrt(jnp.mean(x * x, axis=-1, keepdims=True) + EPS)
        y = ((x * inv) * gmul + shift).astype(BF16)
        dst_ref[pl.ds(s * tt + r, norm_rows), :] = y
        if init_acc:
            o_ref[s, pl.ds(r, norm_rows), :] = x
        return y

    def whole_tile_prologue():
        for s in range(bt):
            gmul = ng_ref[...] * (1 + scale_ref[s])
            shift = shift_ref[s]

            def rows(i, carry, s=s, gmul=gmul, shift=shift):
                norm_chunk(hn_refs[0], s, pl.multiple_of(i * norm_rows, norm_rows), gmul, shift, True)
                return carry

            lax.fori_loop(0, tt // norm_rows, rows, 0, unroll=4)

    @pl.when(j == 0)
    def _():
        @pl.when(is_first)
        def _():
            h_copy(b, t).start()

        if spread:
            @pl.when(is_first)
            def _():
                h_copy(b, t).wait()
                whole_tile_prologue()
        else:
            h_copy(b, t).wait()
            whole_tile_prologue()

    @pl.when((j == jnp.minimum(1, nj - 1)) & not_last)
    def _():
        h_copy(next_b, next_t).start()

    if spread:
        @pl.when((j == 2) & not_last)
        def _():
            h_copy(next_b, next_t).wait()

    def step(norm_next, parity, from_staging=False):
        if emit_bf16_weights:
            wg, wu, wd = (w[...].astype(BF16) for w in (wg_ref, wu_ref, wd_ref))
            wg_out[...], wu_out[...], wd_out[...] = wg, wu, wd
        else:
            wg, wu, wd = wg_ref[...], wu_ref[...], wd_ref[...]
        hn = hn_refs[parity][...]
        gated = _dot(hn, wg)
        if norm_next:
            rows_per_step = tt // spread_steps
            n_sub = rows_per_step // norm_rows
            blk = bt * tt // n_sub
            gmul = ng_ref[...] * (1 + nscale_ref[0])
            shift = nshift_ref[0]
            pieces = []
            for k in range(n_sub):
                r = pl.multiple_of((j - 2) * rows_per_step + k * norm_rows, norm_rows)
                y = norm_chunk(hn_refs[1 - parity], 0, r, gmul, shift, False)
                bits = lax.bitcast_convert_type(y[:, 0:gated.shape[1]].astype(F32), jnp.int32)
                zero = lax.shift_right_logical(lax.shift_right_logical(bits, 16), 16).astype(F32)
                pieces += [gated[k * blk:k * blk + norm_rows] + zero,
                           gated[k * blk + norm_rows:(k + 1) * blk]]
            gated = jnp.concatenate(pieces, axis=0)
        act = (_silu(gated) * _dot(hn, wu)).astype(BF16)
        acc = hbuf_ref[...] if from_staging else o_ref[...]
        o_ref[...] = acc + (0.5 * gate_ref[...]) * _dot(act, wd).reshape(bt, tt, d)

    if spread:
        in_window = (j >= 2) & (j < 2 + spread_steps) & not_last
        for parity in range(2):
            mine = tile % 2 == parity
            staged = (j == 0) & jnp.logical_not(is_first)
            pl.when(mine & in_window)(functools.partial(step, True, parity))
            pl.when(mine & staged)(functools.partial(step, False, parity, True))
            pl.when(mine & jnp.logical_not(in_window | staged))(functools.partial(step, False, parity))
    else:
        step(False, 0)

    if final_norm:
        @pl.when(j == nj - 1)
        def _():
            y = o_ref[...]
            o_ref[...] = (y * lax.rsqrt(jnp.mean(y * y, axis=-1, keepdims=True) + EPS)) * fg_ref[...]


def _ffn_call(h, mod, mod_row, norm_g, weights, final_g, *, bt, tt, tf, final_norm):
    bsz, seq, d = h.shape
    kind, (wg, wu, wd) = weights[0], weights[1]
    ff = wg.shape[-1]
    grid = (bsz // bt, seq // tt, ff // tf)
    mspec = [pl.BlockSpec((bt, 1, d), functools.partial(_mod_index, row=mod_row(k), bt=bt))
             for k in range(3)]
    norm_rows = 16
    assert seq % tt == 0 and bsz % bt == 0 and ff % tf == 0 and tt % (4 * norm_rows) == 0
    spread_steps = FFN_NORM_SPREAD_STEPS
    if not (bt == 1 and grid[0] * grid[1] > 1 and grid[2] >= spread_steps + 2
            and tt % (spread_steps * norm_rows) == 0):
        spread_steps = 0
    next_index = lambda b, t, j: (jnp.where(t == grid[1] - 1, jnp.minimum(b + 1, grid[0] - 1), b),)
    nspec = [pl.BlockSpec((bt, 1, d), lambda b, t, j, k=k: _mod_index(
        *next_index(b, t, j), row=mod_row(k), bt=bt)) for k in range(2)]
    up_spec = pl.BlockSpec((d, tf), lambda b, t, j: (0, j))
    down_spec = pl.BlockSpec((tf, d), lambda b, t, j: (j, 0))
    out_specs = [pl.BlockSpec((bt, tt, d), lambda b, t, j: (b, t, 0))]
    out_shape = [jax.ShapeDtypeStruct(h.shape, F32)]
    if kind == "f32":
        l, i = weights[2], weights[3]
        assert grid[:2] == (1, 1)
        w_specs = [pl.BlockSpec((None, None, d, tf), lambda b, t, j: (l, i, 0, j)),
                   pl.BlockSpec((None, None, d, tf), lambda b, t, j: (l, i, 0, j)),
                   pl.BlockSpec((None, None, tf, d), lambda b, t, j: (l, i, j, 0))]
        out_specs += [up_spec, up_spec, down_spec]
        out_shape += [jax.ShapeDtypeStruct((d, ff), BF16), jax.ShapeDtypeStruct((d, ff), BF16),
                      jax.ShapeDtypeStruct((ff, d), BF16)]
    else:
        w_specs = [up_spec, up_spec, down_spec]
    out = pl.pallas_call(
        functools.partial(_ffn_kernel, final_norm=final_norm, norm_rows=norm_rows,
                          emit_bf16_weights=(kind == "f32"), spread_steps=spread_steps),
        grid=grid,
        in_specs=[
            pl.BlockSpec(memory_space=pl.ANY),
            *mspec,
            *nspec,
            pl.BlockSpec((1, d), lambda b, t, j: (0, 0)),
            *w_specs,
            pl.BlockSpec((1, d), lambda b, t, j: (0, 0)),
        ],
        out_specs=out_specs,
        out_shape=out_shape,
        scratch_shapes=[pltpu.VMEM((bt, tt, d), F32),
                        *[pltpu.VMEM((bt * tt, d), BF16)] * (2 if spread_steps else 1),
                        pltpu.SemaphoreType.DMA(())],
        compiler_params=pltpu.CompilerParams(
            dimension_semantics=("arbitrary", "arbitrary", "arbitrary"),
            vmem_limit_bytes=V7X_VMEM_LIMIT_BYTES),
        name="ffn_half_step",
    )(h, mod, mod, mod, mod, mod, norm_g, wg, wu, wd, final_g)
    return out[0], (tuple(out[1:]) if kind == "f32" else (wg, wu, wd))


def _mod_index(b, *_, row, bt):
    return (row // bt + b, 0, 0)


def _mixer_ab_front(t, last_t, h_ref, shift_ref, scale_ref, ng_ref, hist_ref, win_ref, vg_ref, vb_ref,
                    ws_ref, bst_ref, dw_ref, lng_ref, lnb_ref, convnew_ref, vn_ref, xpad_ref, conv_ref,
                    xsh_ref, dwb_ref, aout_ref, bout_ref, hres_ref, *, conv_rows, conv_cols,
                    unroll_conv, midway=None):
    bt, tt, d = h_ref.shape
    aw = vg_ref.shape[-1]
    bw = lng_ref.shape[-1]
    n_grp, a_chunk, _ = ws_ref.shape
    gd = aw // n_grp
    n_taps = dw_ref.shape[0]
    hpad = xpad_ref.shape[1] - tt
    off = hpad - (n_taps - 1)
    cr = min(a_chunk, tt)

    @pl.when(t == 0)
    def _():
        xpad_ref[:, 0:hpad, :] = hist_ref[...]
        for k in range(n_taps):
            dwb_ref[k] = jnp.broadcast_to(dw_ref[k:k + 1, :], (V7X_SUBLANES, bw))

    h = h_ref[...]
    hn = _rms_mod(h, ng_ref[...], shift_ref[...], scale_ref[...]).astype(BF16).reshape(bt * tt, d)

    ba = _dot(hn, win_ref[:, 2 * aw:2 * aw + bw])
    bg = _dot(hn, win_ref[:, 2 * aw + bw:2 * aw + 2 * bw])
    xpad_ref[:, hpad:hpad + tt, :] = (ba * _sigmoid(bg)).reshape(bt, tt, bw)

    n_sh = xsh_ref.shape[1]

    def conv_block(s, r0, cols):
        tiles = conv_rows // V7X_SUBLANES
        acc = jnp.zeros((tiles, V7X_SUBLANES, conv_cols), F32)
        for k in range(n_taps):
            r = (off + k) % V7X_SUBLANES
            a = (off + k) - r
            if r == 0:
                x = xpad_ref[s, pl.ds(r0 + a, conv_rows), cols]
            else:
                x = xsh_ref[r - 1, pl.ds(r0 + a, conv_rows), cols]
            acc = acc + x.reshape(tiles, V7X_SUBLANES, conv_cols) * dwb_ref[k, :, cols]
        conv_ref[pl.ds(s * tt + r0, conv_rows), cols] = acc.reshape(conv_rows, conv_cols)

    for s in range(bt):
        for r in range(1, V7X_SUBLANES):
            xsh_ref[r - 1] = xpad_ref[s, r:r + n_sh, :]
        for c0 in range(0, bw, conv_cols):
            cols = slice(c0, c0 + conv_cols)
            if unroll_conv:
                for r0 in range(0, tt, conv_rows):
                    conv_block(s, r0, cols)
            else:
                def conv_body(rb, carry, s=s, cols=cols):
                    conv_block(s, pl.multiple_of(rb * conv_rows, conv_rows), cols)
                    return carry

                lax.fori_loop(0, tt // conv_rows, conv_body, 0)

    u = jax.nn.gelu(_dot(hn, win_ref[:, 0:aw]))
    v = jax.nn.gelu(_dot(hn, win_ref[:, aw:2 * aw]))
    tie = midway(v) if midway is not None else None
    vn = _layer_norm(v, vg_ref[...], vb_ref[...])
    if vn_ref is not None:
        vn_ref[...] = vn.reshape(bt, tt, aw)
    vnb = vn.astype(BF16)
    gate_rows = min(V7X_MXU_DIM, bt * tt)
    row = lax.broadcasted_iota(jnp.int32, (a_chunk, a_chunk), 0)
    col = lax.broadcasted_iota(jnp.int32, (a_chunk, a_chunk), 1)
    keep = (row < cr) & (col < cr) & (col // STREAM_CHUNK <= row // STREAM_CHUNK)
    for g in range(n_grp):
        corner = jnp.where(keep, ws_ref[g], 0.0)
        diag = corner
        for i in range(1, a_chunk // cr):
            diag = diag + pltpu.roll(pltpu.roll(corner, i * cr, 0), i * cr, 1)
        diag = diag.astype(BF16)
        nb = gate_rows // a_chunk
        zero = jnp.zeros_like(diag)
        wbd = diag if nb <= 1 else jnp.concatenate(
            [jnp.concatenate([diag if i == j else zero for j in range(nb)], axis=1)
             for i in range(nb)], axis=0)
        wbd = wbd[0:gate_rows, 0:gate_rows]
        bias = jnp.concatenate([bst_ref[0:cr, g:g + 1]] * (gate_rows // cr), axis=0)
        for r0 in range(0, bt * tt, gate_rows):
            gate = _dot(wbd, vnb[r0:r0 + gate_rows, g * gd:(g + 1) * gd]) + bias
            aout_ref[r0:r0 + gate_rows, g * gd:(g + 1) * gd] = (
                u[r0:r0 + gate_rows, g * gd:(g + 1) * gd] * gate).astype(BF16)

    conv = conv_ref[...]
    if tie is not None:
        tie_row = jnp.concatenate([tie] * (bw // V7X_LANES), axis=1)
        conv = (conv.reshape(bt * tt // V7X_SUBLANES, V7X_SUBLANES, bw) + tie_row).reshape(bt * tt, bw)
    bout_ref[...] = _silu(_layer_norm(conv, lng_ref[...], lnb_ref[...])).astype(BF16)
    if hres_ref is not None:
        hres_ref[...] = h

    tail = xpad_ref[:, tt:tt + hpad, :]
    xpad_ref[:, 0:hpad, :] = tail

    @pl.when(last_t)
    def _():
        convnew_ref[...] = tail


def _exact_zero_tile(x):
    bits = lax.bitcast_convert_type(x[0:V7X_SUBLANES, 0:V7X_LANES].astype(F32), jnp.int32)
    return lax.shift_right_logical(lax.shift_right_logical(bits, 16), 16).astype(F32)


def _mixer_ab_back(gate_ref, wout_ref, o_ref, aout_ref, bout_ref, hres_ref, tie_rows=32, after=None):
    bt, tt, d = o_ref.shape
    aw, bw = aout_ref.shape[-1], bout_ref.shape[-1]
    aout = aout_ref[...]
    if after is not None:
        n_r, n_c = after.shape[0] - V7X_SUBLANES, after.shape[1] - V7X_LANES
        z = sum(_exact_zero_tile(after[r:, c:]) for r in (0, n_r) for c in (0, n_c))
        z = jnp.concatenate([jnp.concatenate([z, z], axis=0)] * (aw // V7X_LANES), axis=1).astype(BF16)
        rows = z.shape[0]
        aout = (aout.reshape(bt * tt // rows, rows, aw) + z).reshape(bt * tt, aw)
    y = _dot(aout, wout_ref[0:aw, :]) + _dot(bout_ref[...], wout_ref[aw:aw + bw, :])
    o_ref[...] = hres_ref[...] + gate_ref[...] * y.reshape(bt, tt, d)
    zero = jnp.zeros((V7X_SUBLANES, V7X_LANES), jnp.int32)
    for r0 in range(0, bt * tt, tie_rows):
        for c0 in range(0, d, V7X_MXU_DIM):
            bits = lax.bitcast_convert_type(y[r0:r0 + V7X_SUBLANES, c0:c0 + V7X_LANES], jnp.int32)
            zero = zero + lax.shift_right_logical(lax.shift_right_logical(bits, 16), 16)
    return zero.astype(F32)


def _mixer_ab_kernel(h_ref, shift_ref, scale_ref, gate_ref, ng_ref, hist_ref, win_ref, vg_ref,
                     vb_ref, ws_ref, bst_ref, dw_ref, lng_ref, lnb_ref, wout_ref,
                     o_ref, convnew_ref, *rest, emit_vn, n_tiles, tiles_per_stream, **conv_params):
    vn_ref = rest[0] if emit_vn else None
    xpad_ref, conv_ref, xsh_ref, dwb_ref, *handover = rest[1:] if emit_vn else rest
    if n_tiles > 1:
        handover = (handover[0:3], handover[3:6])
    else:
        handover = ((*handover, None),)
    q = pl.program_id(0)
    t = q % tiles_per_stream

    def front(parity, midway=None):
        _mixer_ab_front(t, t == tiles_per_stream - 1, h_ref, shift_ref, scale_ref, ng_ref, hist_ref,
                        win_ref, vg_ref, vb_ref, ws_ref, bst_ref, dw_ref, lng_ref, lnb_ref, convnew_ref,
                        vn_ref, xpad_ref, conv_ref, xsh_ref, dwb_ref, *handover[parity], midway=midway,
                        **conv_params)

    def back(parity, after=None):
        aout_ref, bout_ref, hres_ref = handover[parity]
        return _mixer_ab_back(gate_ref, wout_ref, o_ref, aout_ref, bout_ref,
                              h_ref if hres_ref is None else hres_ref, after=after)

    @pl.when(q == 0)
    def _():
        front(0)

    if n_tiles > 1:
        for parity in range(2):
            @pl.when((q > 0) & (q < n_tiles) & (q % 2 == parity))
            def _(parity=parity):
                front(parity, midway=functools.partial(back, 1 - parity))

    @pl.when(q == n_tiles)
    def _():
        back((n_tiles - 1) % 2)


def _mixer_ab_call(h, mod, mod_row, norm_g, hist, w_in, v_g, v_b, ws, bs_t, dw, ln_g, ln_b, w_out,
                   *, bt, tt, emit_vn):
    bsz, seq, d = h.shape
    aw, bw = v_g.shape[-1], ln_g.shape[-1]
    hpad = hist.shape[1]
    conv_rows = min(tt, 32)
    conv_cols = 256
    unroll_conv = bt == 1 and (bw // conv_cols) * (tt // conv_rows) <= MAX_UNROLLED_CONV_BLOCKS
    once = dict(pipeline_mode=pl.Buffered(1))
    full2 = lambda a: pl.BlockSpec(a.shape, lambda q: (0, 0))
    nt = seq // tt
    n_tiles = (bsz // bt) * nt
    front_tile = lambda q: jnp.minimum(q, n_tiles - 1)
    back_tile = lambda q: jnp.maximum(q - 1, 0)
    stream_of = lambda tile: tile // nt
    front_mod = [pl.BlockSpec((bt, 1, d), lambda q, k=k: _mod_index(
        stream_of(front_tile(q)), row=mod_row(k), bt=bt)) for k in range(2)]
    back_mod = pl.BlockSpec((bt, 1, d), lambda q: _mod_index(
        stream_of(back_tile(q)), row=mod_row(2), bt=bt))
    front_rows = lambda q: (stream_of(front_tile(q)), front_tile(q) % nt, 0)
    out_specs = [pl.BlockSpec((bt, tt, d), lambda q: (stream_of(back_tile(q)), back_tile(q) % nt, 0)),
                 pl.BlockSpec((bt, hpad, bw), lambda q: (stream_of(front_tile(q)), 0, 0))]
    out_shape = [jax.ShapeDtypeStruct(h.shape, F32),
                 jax.ShapeDtypeStruct((bsz, hpad, bw), F32)]
    if emit_vn:
        out_specs.append(pl.BlockSpec((bt, tt, aw), front_rows))
        out_shape.append(jax.ShapeDtypeStruct((bsz, seq, aw), F32))
    single = once if n_tiles == 1 else {}
    handover = [pltpu.VMEM((bt * tt, aw), BF16), pltpu.VMEM((bt * tt, bw), BF16)]
    if n_tiles > 1:
        handover = 2 * (handover + [pltpu.VMEM((bt, tt, d), F32)])
    return pl.pallas_call(
        functools.partial(_mixer_ab_kernel, emit_vn=emit_vn, n_tiles=n_tiles, tiles_per_stream=nt,
                          conv_rows=conv_rows, conv_cols=conv_cols, unroll_conv=unroll_conv),
        grid=(n_tiles + 1,),
        in_specs=[
            pl.BlockSpec((bt, tt, d), front_rows, **single),
            *front_mod, back_mod,
            full2(norm_g),
            pl.BlockSpec((bt, hpad, bw), lambda q: (stream_of(front_tile(q)), 0, 0), **single),
            pl.BlockSpec(w_in.shape, lambda q: (0, 0), **once),
            full2(v_g), full2(v_b),
            pl.BlockSpec(ws.shape, lambda q: (0, 0, 0)),
            full2(bs_t), full2(dw), full2(ln_g), full2(ln_b),
            pl.BlockSpec(w_out.shape, lambda q: (0, 0), **once),
        ],
        out_specs=out_specs,
        out_shape=out_shape,
        scratch_shapes=[pltpu.VMEM((bt, hpad + tt, bw), F32),
                        pltpu.VMEM((bt * tt, bw), F32),
                        pltpu.VMEM((V7X_SUBLANES - 1, hpad + tt - V7X_SUBLANES, bw), F32),
                        pltpu.VMEM((dw.shape[0], V7X_SUBLANES, bw), F32),
                        *handover],
        compiler_params=pltpu.CompilerParams(
            dimension_semantics=("arbitrary",),
            vmem_limit_bytes=V7X_VMEM_LIMIT_BYTES),
        name="mixer_ab",
    )(h, mod, mod, mod, norm_g, hist, w_in, v_g, v_b, ws, bs_t, dw, ln_g, ln_b, w_out)


def _mixer_c_kernel(h_ref, shift_ref, scale_ref, gate_ref, ng_ref, hist_ref, wgrp_ref, bgrp_ref,
                    cs_ref, o_ref, poolnew_ref, xpad_ref, y_ref, lvl_ref, *, start_pos):
    t = pl.program_id(1)
    bt, tt, d = h_ref.shape
    n_grp = wgrp_ref.shape[0]
    gd = d // n_grp
    hpad = xpad_ref.shape[1] - tt

    @pl.when(t == 0)
    def _():
        xpad_ref[:, 0:hpad, :] = hist_ref[...]

    h = h_ref[...]
    hn = _rms_mod(h, ng_ref[...], shift_ref[...], scale_ref[...])
    xpad_ref[:, hpad:hpad + tt, :] = hn

    end = hpad + tt

    def window_sum(cols, w):
        m = w.bit_length() - 1
        assert w == 1 << m and hpad - w + 1 >= 0

        def read(level, row0, n):
            if level == 0:
                return xpad_ref[:, row0:row0 + n, cols]
            return lvl_ref[(level - 1) % 2, :, row0:row0 + n, :]

        for i in range(1, m):
            k = 1 << (i - 1)
            lo = (hpad - w + (1 << i)) // V7X_SUBLANES * V7X_SUBLANES
            buf = (i - 1) % 2
            if lo > 0:
                lvl_ref[buf, :, 0:lo, :] = jnp.zeros((bt, lo, gd), F32)
            lvl_ref[buf, :, lo:end, :] = read(i - 1, lo, end - lo) + read(i - 1, lo - k, end - lo)
        k = 1 << (m - 1)
        return read(m - 1, hpad, tt) + read(m - 1, hpad - k, tt)

    pos = start_pos + t * tt + lax.broadcasted_iota(jnp.int32, (1, tt, 1), 1)
    for g, w in enumerate(POOL_WINDOWS):
        cols = slice(g * gd, (g + 1) * gd)
        s = window_sum(cols, w)
        cnt = jnp.minimum(w, pos + 1).astype(F32)
        pooled = (s / cnt - hn[:, :, cols]).astype(BF16).reshape(bt * tt, gd)
        y_ref[:, cols] = _dot(pooled, wgrp_ref[g])
    y = (y_ref[...] + bgrp_ref[...]) * cs_ref[...]
    o_ref[...] = h + gate_ref[...] * y.reshape(bt, tt, d)

    tail = xpad_ref[:, tt:tt + hpad, :]
    xpad_ref[:, 0:hpad, :] = tail

    @pl.when(t == pl.num_programs(1) - 1)
    def _():
        poolnew_ref[...] = tail


def _mixer_c_call(h, mod, mod_row, norm_g, hist, w_grp, b_grp, c_scale, *, bt, tt, start_pos):
    bsz, seq, d = h.shape
    hpad = hist.shape[1]
    full2 = lambda a: pl.BlockSpec(a.shape, lambda b, t: (0, 0))
    mspec = [pl.BlockSpec((bt, 1, d), functools.partial(_mod_index, row=mod_row(k), bt=bt))
             for k in range(3)]
    return pl.pallas_call(
        functools.partial(_mixer_c_kernel, start_pos=start_pos),
        grid=(bsz // bt, seq // tt),
        in_specs=[
            pl.BlockSpec((bt, tt, d), lambda b, t: (b, t, 0)),
            *mspec,
            full2(norm_g),
            pl.BlockSpec((bt, hpad, d), lambda b, t: (b, 0, 0)),
            pl.BlockSpec(w_grp.shape, lambda b, t: (0, 0, 0)),
            full2(b_grp), full2(c_scale),
        ],
        out_specs=[pl.BlockSpec((bt, tt, d), lambda b, t: (b, t, 0)),
                   pl.BlockSpec((bt, hpad, d), lambda b, t: (b, 0, 0))],
        out_shape=[jax.ShapeDtypeStruct(h.shape, F32),
                   jax.ShapeDtypeStruct((bsz, hpad, d), F32)],
        scratch_shapes=[pltpu.VMEM((bt, hpad + tt, d), F32),
                        pltpu.VMEM((bt * tt, d), F32),
                        pltpu.VMEM((2, bt, hpad + tt, d // w_grp.shape[0]), F32)],
        compiler_params=pltpu.CompilerParams(
            dimension_semantics=("arbitrary", "arbitrary"),
            vmem_limit_bytes=V7X_VMEM_LIMIT_BYTES),
        name="mixer_c",
    )(h, mod, mod, mod, norm_g, hist, w_grp, b_grp, c_scale)


def _front_pad(hist, rows):
    return jnp.pad(hist, ((0, 0), (0, 0), (rows - hist.shape[2], 0), (0, 0)))


def _run_trunk(x, mod, row0, start_pos, conv_hist, pool_hist, p, ffn_weights, *, bt, tt_ffn,
               bt_mix, tt_mix, tt_pool, tf, emit_vn):
    n_layers = p["norm_g"].shape[0]
    n_rows = mod.shape[2]
    d = x.shape[-1]
    mod_tab = mod.reshape(n_layers * 9 * n_rows, 1, d)
    n_conv = conv_hist.shape[2]
    n_pool = pool_hist.shape[2]
    conv_hist = _front_pad(conv_hist, _round_up(n_conv, V7X_SUBLANES))
    pool_hist = _front_pad(pool_hist, _round_up(n_pool, V7X_SUBLANES) + V7X_SUBLANES)

    h = x
    new_conv, new_pool, new_av, bf16_weights = [], [], [], []
    for l in range(n_layers):
        def mod_row(sub, l=l):
            return lambda k: ((l * 9 + sub * 3 + k) * n_rows + row0)

        ffn = functools.partial(_ffn_call, final_g=p["final_norm_g"], bt=bt, tt=tt_ffn, tf=tf)
        h, w_first = ffn(h, mod_tab, mod_row(0), p["norm_g"][l, 0:1], ffn_weights[l][0],
                         final_norm=False)
        if l % 2 == 0:
            e = l // 2
            out = _mixer_ab_call(
                h, mod_tab, mod_row(1), p["norm_g"][l, 1:2], conv_hist[e], p["ab_w_in"][e],
                p["a_v_norm_g"][e:e + 1], p["a_v_norm_b"][e:e + 1], p["a_ws"][e], p["a_bs"][e].T,
                p["b_dw"][e], p["b_ln_g"][e:e + 1], p["b_ln_b"][e:e + 1], p["ab_w_out"][e],
                bt=bt_mix, tt=tt_mix, emit_vn=emit_vn)
            h = out[0]
            new_conv.append(out[1][:, -n_conv:])
            if emit_vn:
                new_av.append(out[2])
        else:
            o = l // 2
            h, ph = _mixer_c_call(
                h, mod_tab, mod_row(1), p["norm_g"][l, 1:2], pool_hist[o], p["c_w_grp"][o],
                p["c_b_grp"][o].reshape(1, d), p["c_scale"][o:o + 1],
                bt=bt, tt=tt_pool, start_pos=start_pos)
            new_pool.append(ph[:, -n_pool:])
        h, w_second = ffn(h, mod_tab, mod_row(2), p["norm_g"][l, 2:3], ffn_weights[l][1],
                          final_norm=(l == n_layers - 1))
        bf16_weights.append((("bf16", w_first), ("bf16", w_second)))
    outs = (h, jnp.stack(new_conv), jnp.stack(new_pool), jnp.stack(new_av) if emit_vn else None)
    return outs, bf16_weights


def kernel(x_prompt, x_sample, c_prompt, c_sample, state_conv_b, state_pool_c, ada_w, ada_b, norm_g, final_norm_g, ffn_w_gate, ffn_w_up, ffn_w_down, ab_w_in, a_v_norm_g, a_v_norm_b, a_ws, a_bs, b_dw, b_ln_g, b_ln_b, ab_w_out, c_w_grp, c_b_grp, c_scale):
    bp, sp, d = x_prompt.shape
    bs, ss, _ = x_sample.shape
    n_layers = norm_g.shape[0]
    p = {
        "norm_g": norm_g, "final_norm_g": final_norm_g.reshape(1, d),
        "ab_w_in": ab_w_in.astype(BF16), "ab_w_out": ab_w_out.astype(BF16),
        "a_v_norm_g": a_v_norm_g, "a_v_norm_b": a_v_norm_b, "a_ws": a_ws, "a_bs": a_bs,
        "b_dw": b_dw, "b_ln_g": b_ln_g, "b_ln_b": b_ln_b,
        "c_w_grp": c_w_grp.astype(BF16), "c_b_grp": c_b_grp, "c_scale": c_scale,
    }
    n_rows = _round_up(bs + bp, V7X_SUBLANES)
    c_all = jnp.concatenate([c_sample, c_prompt, jnp.zeros((n_rows - bs - bp, d), F32)], axis=0)
    mod = _ada_call(c_all, ada_w, ada_b)

    f32_weights = [[("f32", (ffn_w_gate, ffn_w_up, ffn_w_down), l, i) for i in range(2)]
                   for l in range(n_layers)]
    (y_s, conv_s, pool_s, av_s), bf16_weights = _run_trunk(
        x_sample, mod, 0, PAST_LEN, state_conv_b, state_pool_c, p, f32_weights,
        bt=bs, tt_ffn=ss, bt_mix=min(bs, SAMPLE_MIXER_STREAMS), tt_mix=ss, tt_pool=ss,
        tf=SAMPLE_FFN_HIDDEN_TILE, emit_vn=True)

    n_even, n_odd = state_conv_b.shape[0], state_pool_c.shape[0]
    conv0 = jnp.zeros((n_even, bp) + state_conv_b.shape[2:], F32)
    pool0 = jnp.zeros((n_odd, bp) + state_pool_c.shape[2:], F32)
    (y_p, conv_p, pool_p, _), _ = _run_trunk(
        x_prompt, mod, bs, 0, conv0, pool0, p, bf16_weights,
        bt=1, tt_ffn=min(sp, PROMPT_FFN_ROWS), bt_mix=1, tt_mix=min(sp, PROMPT_MIXER_ROWS),
        tt_pool=min(sp, PROMPT_POOL_ROWS), tf=FFN_HIDDEN_TILE, emit_vn=False)
    return (y_p, y_s, conv_p, conv_s, pool_p, pool_s, av_s)
```
